```python
import jax, jax.numpy as jnp
from jax import lax
import numpy as np

D_MODEL = 2048
BATCH = 1
SEQ = 16384
DEPTH = 1
DEC_BATCH = 8
DEC_SEQ = 64
PAST_LEN = 1024

CHUNK = 64
R_HEADS = 8
R_DK = 128
R_DV = D_MODEL // R_HEADS
A_HEADS = 16
A_KV_HEADS = 4
A_DH = D_MODEL // A_HEADS
A_GROUP = A_HEADS // A_KV_HEADS
I_HEADS = 16
I_DH = 64
TOPK_MAX = 256
Q_BLOCK = 128
D_FF = 5632
CONV_W = 3
ALPHA = (2 * DEPTH) ** 0.25
BETA = (8 * DEPTH) ** -0.25
LN_EPS = 1e-5
SPLITS = (R_HEADS * R_DK, R_HEADS * R_DK, D_MODEL, D_MODEL,
          A_HEADS * A_DH, A_KV_HEADS * A_DH, A_KV_HEADS * A_DH,
          I_HEADS * I_DH, I_DH, I_HEADS,
          D_MODEL, D_MODEL)
D_IN = sum(SPLITS)
V_COLS = (2, 6)

kernel_name = "retention_dsa_convffn_deepnorm_step"


def _split_offsets():
    return tuple(int(o) for o in np.cumsum(SPLITS)[:-1])


def _layernorm(x, g, b):
    xf = x.astype(jnp.float32)
    mu = jnp.mean(xf, -1, keepdims=True)
    var = jnp.mean(jnp.square(xf - mu), -1, keepdims=True)
    y = (xf - mu) * lax.rsqrt(var + LN_EPS) * g.astype(jnp.float32) + b.astype(jnp.float32)
    return y.astype(x.dtype)


def _log_gamma():
    return jnp.log1p(-jnp.exp2(-5.0 - jnp.arange(R_HEADS, dtype=jnp.float32)))


def _alibi_slopes():
    return jnp.exp2(-8.0 * (jnp.arange(A_HEADS, dtype=jnp.float32) + 1.0) / A_HEADS)


def _retention_chunk(S, qkv):
    q, k, v = qkv
    C = q.shape[1]
    lg = _log_gamma()
    i = jnp.arange(C, dtype=jnp.float32)
    diff = i[:, None] - i[None, :]
    decay = jnp.where(diff >= 0, jnp.exp(lg[:, None, None] * jnp.maximum(diff, 0.0)), 0.0)
    scores = jnp.einsum('bihd,bjhd->bhij', q, k) * decay[None]
    inner = jnp.einsum('bhij,bjhe->bihe', scores, v)
    cross_decay = jnp.exp((i + 1.0)[:, None] * lg[None, :])
    cross = jnp.einsum('bihd,bhde->bihe', q, S) * cross_decay[None, :, :, None]
    k_decay = jnp.exp((C - 1.0 - i)[:, None] * lg[None, :])
    S_new = (jnp.exp(C * lg)[None, :, None, None] * S
             + jnp.einsum('bjhd,bjhe->bhde', k * k_decay[None, :, :, None], v))
    return S_new, inner + cross


def _retention_out(o, g, gn_g):
    B, T = o.shape[0], o.shape[1]
    mu = jnp.mean(o, -1, keepdims=True)
    var = jnp.mean(jnp.square(o - mu), -1, keepdims=True)
    on = ((o - mu) * lax.rsqrt(var + LN_EPS)).reshape(B, T, D_MODEL) * gn_g.astype(jnp.float32)
    return jax.nn.silu(g.astype(jnp.float32)) * on


def _dsa_block(q, qi, wi, q_pos, k_all, v_all, ki_all, n_sel):
    f32 = jnp.float32
    B, Q = q.shape[0], q.shape[1]
    L = k_all.shape[1]
    idx_logits = jnp.einsum('bqhd,bsd->bqhs', qi.astype(f32), ki_all.astype(f32))
    w = wi.astype(f32) * (I_DH ** -0.5 * I_HEADS ** -0.5)
    iscore = jnp.einsum('bqh,bqhs->bqs', w, jax.nn.relu(idx_logits))
    limit = jnp.minimum((q_pos // CHUNK + 1) * CHUNK, L)
    admissible = jnp.arange(L, dtype=jnp.int32)[None, :] < limit[:, None]
    iscore = jnp.where(admissible[None], iscore, -jnp.inf)
    _, sel = lax.top_k(iscore, n_sel)
    k_sel = jax.vmap(lambda kb, ib: kb[ib])(k_all, sel)
    v_sel = jax.vmap(lambda vb, ib: vb[ib])(v_all, sel)
    valid = sel < limit[None, :, None]
    qg = q.astype(f32).reshape(B, Q, A_KV_HEADS, A_GROUP, A_DH)
    s = jnp.einsum('bqhgd,bqjhd->bqhgj', qg, k_sel.astype(f32)) * (A_DH ** -0.5)
    dist = jnp.abs(q_pos[None, :, None] - sel).astype(f32)
    slopes = _alibi_slopes().reshape(A_KV_HEADS, A_GROUP)
    s = s - slopes[None, None, :, :, None] * dist[:, :, None, None, :]
    s = jnp.where(valid[:, :, None, None, :], s, -jnp.inf)
    p = jax.nn.softmax(s, axis=-1)
    o = jnp.einsum('bqhgj,bqjhd->bqhgd', p, v_sel.astype(f32))
    return o.reshape(B, Q, D_MODEL)


def _layer(x, past_k, past_v, past_ki, ret_state, conv_buf,
           w_in, b_gate, gn_g, w_out, ln1_g, ln1_b, w_up, conv_w, conv_b, w_down, ln2_g, ln2_b):
    f32 = jnp.float32
    B, T, _ = x.shape
    P = past_k.shape[1]
    L = P + T
    proj = x @ w_in
    (rq, rk, rv, rg, aq, ak, av, iq, ik, iw, g_r, g_a) = jnp.split(proj, _split_offsets(), axis=-1)

    q = rq.reshape(B, T, R_HEADS, R_DK).astype(f32)
    k = rk.reshape(B, T, R_HEADS, R_DK).astype(f32) * (R_DK ** -0.5)
    v = rv.reshape(B, T, R_HEADS, R_DV).astype(f32)
    s0 = ret_state.astype(f32)
    if T > CHUNK:
        nc = T // CHUNK
        chunks = tuple(jnp.moveaxis(t.reshape(B, nc, CHUNK, *t.shape[2:]), 1, 0) for t in (q, k, v))
        s_new, o = lax.scan(_retention_chunk, s0, chunks)
        o = jnp.moveaxis(o, 0, 1).reshape(B, T, R_HEADS, R_DV)
    else:
        s_new, o = _retention_chunk(s0, (q, k, v))
    y_r = _retention_out(o, rg, gn_g)

    k_new = ak.reshape(B, T, A_KV_HEADS, A_DH)
    v_new = av.reshape(B, T, A_KV_HEADS, A_DH)
    k_all = jnp.concatenate([past_k, k_new], axis=1)
    v_all = jnp.concatenate([past_v, v_new], axis=1)
    ki_all = jnp.concatenate([past_ki, ik], axis=1)
    n_sel = min(TOPK_MAX, L // 4)
    pos = P + jnp.arange(T, dtype=jnp.int32)
    qa = aq.reshape(B, T, A_HEADS, A_DH)
    qi = iq.reshape(B, T, I_HEADS, I_DH)
    if T > Q_BLOCK:
        nb = T // Q_BLOCK
        blk = lambda t: jnp.moveaxis(t.reshape(B, nb, Q_BLOCK, *t.shape[2:]), 1, 0)
        o_a = lax.map(lambda a: _dsa_block(a[0], a[1], a[2], a[3], k_all, v_all, ki_all, n_sel),
                      (blk(qa), blk(qi), blk(iw), pos.reshape(nb, Q_BLOCK)))
        y_a = jnp.moveaxis(o_a, 0, 1).reshape(B, T, D_MODEL)
    else:
        y_a = _dsa_block(qa, qi, iw, pos, k_all, v_all, ki_all, n_sel)

    b_r, b_a = b_gate[:D_MODEL], b_gate[D_MODEL:]
    mix = (jax.nn.sigmoid((g_r + b_r).astype(f32)) * y_r
           + jax.nn.sigmoid((g_a + b_a).astype(f32)) * y_a)
    x1 = _layernorm(ALPHA * x + mix.astype(x.dtype) @ w_out, ln1_g, ln1_b)

    h, u = jnp.split(x1 @ w_up, 2, axis=-1)
    hp = jnp.concatenate([conv_buf.astype(h.dtype), h], axis=1)
    hc = conv_b + sum(conv_w[j] * hp[:, j:j + T] for j in range(CONV_W))
    new_buf = hp[:, T:]
    ff = (jax.nn.gelu(hc.astype(f32)) * u.astype(f32)).astype(x.dtype) @ w_down
    x2 = _layernorm(ALPHA * x1 + ff, ln2_g, ln2_b)
    return x2, (k_new, v_new, ik, s_new.astype(x.dtype), new_buf)


def setup_inputs(seed: int = 0) -> dict:
    key = jax.random.key(seed)
    ks = jax.random.split(key, 20)
    f32 = jnp.float32
    nrm = lambda k, s, sc: jax.random.normal(k, s, f32) * sc
    col_scale = jnp.concatenate([jnp.full((n,), BETA if i in V_COLS else 1.0, f32)
                                 for i, n in enumerate(SPLITS)])
    return {
        "x_prompt": nrm(ks[0], (BATCH, SEQ, D_MODEL), 1.0),
        "x_sample": nrm(ks[1], (DEC_BATCH, DEC_SEQ, D_MODEL), 1.0),
        "cache_attn_k": nrm(ks[2], (DEPTH, DEC_BATCH, PAST_LEN, A_KV_HEADS, A_DH), 1.0),
        "cache_attn_v": nrm(ks[3], (DEPTH, DEC_BATCH, PAST_LEN, A_KV_HEADS, A_DH), BETA),
        "cache_idx_k": nrm(ks[4], (DEPTH, DEC_BATCH, PAST_LEN, I_DH), 1.0),
        "state_ret": nrm(ks[5], (DEPTH, DEC_BATCH, R_HEADS, R_DK, R_DV), 0.3),
        "state_conv": nrm(ks[6], (DEPTH, DEC_BATCH, CONV_W - 1, D_FF), 1.0),
        "w_in": nrm(ks[7], (DEPTH, D_MODEL, D_IN), D_MODEL ** -0.5) * col_scale,
        "b_gate": nrm(ks[8], (DEPTH, 2 * D_MODEL), 0.01),
        "gn_g": 1.0 + nrm(ks[9], (DEPTH, D_MODEL), 0.02),
        "w_out": nrm(ks[10], (DEPTH, D_MODEL, D_MODEL), BETA * D_MODEL ** -0.5),
        "ln1_g": 1.0 + nrm(ks[11], (DEPTH, D_MODEL), 0.02),
        "ln1_b": nrm(ks[12], (DEPTH, D_MODEL), 0.02),
        "w_up": nrm(ks[13], (DEPTH, D_MODEL, 2 * D_FF), D_MODEL ** -0.5),
        "conv_w": nrm(ks[14], (DEPTH, CONV_W, D_FF), CONV_W ** -0.5),
        "conv_b": nrm(ks[15], (DEPTH, D_FF), 0.01),
        "w_down": nrm(ks[16], (DEPTH, D_FF, D_MODEL), BETA * D_FF ** -0.5),
        "ln2_g": 1.0 + nrm(ks[17], (DEPTH, D_MODEL), 0.02),
        "ln2_b": nrm(ks[18], (DEPTH, D_MODEL), 0.02),
    }


def reference(x_prompt, x_sample, cache_attn_k, cache_attn_v, cache_idx_k, state_ret, state_conv,
              w_in, b_gate, gn_g, w_out, ln1_g, ln1_b, w_up, conv_w, conv_b, w_down, ln2_g, ln2_b):
    B = x_prompt.shape[0]
    dt = x_prompt.dtype
    xp, xs = x_prompt, x_sample
    kp, vp, kip, rp, cp = [], [], [], [], []
    ksl, vsl, kisl, rsl, csl = [], [], [], [], []
    for l in range(DEPTH):
        wl = (w_in[l], b_gate[l], gn_g[l], w_out[l], ln1_g[l], ln1_b[l],
              w_up[l], conv_w[l], conv_b[l], w_down[l], ln2_g[l], ln2_b[l])
        xp, st_p = _layer(xp,
                          jnp.zeros((B, 0, A_KV_HEADS, A_DH), dt),
                          jnp.zeros((B, 0, A_KV_HEADS, A_DH), dt),
                          jnp.zeros((B, 0, I_DH), dt),
                          jnp.zeros((B, R_HEADS, R_DK, R_DV), dt),
                          jnp.zeros((B, CONV_W - 1, D_FF), dt),
                          *wl)
        xs, st_s = _layer(xs, cache_attn_k[l], cache_attn_v[l], cache_idx_k[l],
                          state_ret[l], state_conv[l], *wl)
        kp.append(st_p[0]); vp.append(st_p[1]); kip.append(st_p[2]); rp.append(st_p[3]); cp.append(st_p[4])
        ksl.append(st_s[0]); vsl.append(st_s[1]); kisl.append(st_s[2]); rsl.append(st_s[3]); csl.append(st_s[4])
    return (xp, xs,
            jnp.stack(kp), jnp.stack(vp), jnp.stack(kip), jnp.stack(rp), jnp.stack(cp),
            jnp.stack(ksl), jnp.stack(vsl), jnp.stack(kisl), jnp.stack(rsl), jnp.stack(csl))
```

```python
import functools

import numpy as np
import jax
import jax.numpy as jnp
from jax import lax
from jax.experimental import pallas as pl
from jax.experimental.pallas import tpu as pltpu

f32 = jnp.float32
bf16 = jnp.bfloat16

D_MODEL = 2048
CHUNK = 64
R_HEADS = 8
R_DK = 128
R_DV = 256
A_HEADS = 16
A_KV_HEADS = 4
A_DH = 128
A_GROUP = 4
I_HEADS = 16
I_DH = 64
TOPK_MAX = 256
D_FF = 5632
CONV_W = 3
DEPTH = 1
ALPHA = (2 * DEPTH) ** 0.25
LN_EPS = 1e-5

LANES = 128
VMEM_LIMIT = 56 * 1024 * 1024
NEG_BIG = -1e30
F32_LOWEST = -3.4028234663852886e38
INT_MIN = -2 ** 31


def _pick(n, cands):
    for c in cands:
        if n % c == 0:
            return c
    raise ValueError(f"no tile in {cands} divides {n}")


def _cparams(sem):
    return pltpu.CompilerParams(dimension_semantics=sem, vmem_limit_bytes=VMEM_LIMIT)


def _sigmoid(x):
    return 1.0 / (1.0 + jnp.exp(-x))


def _mm_kernel(x_ref, w_ref, s_ref, o_ref):
    acc = jnp.dot(x_ref[...], w_ref[...], preferred_element_type=f32)
    o_ref[...] = (acc * s_ref[...]).astype(o_ref.dtype)


def _matmul(x, w, scale, out_dtype, tn):
    M, K = x.shape
    N = w.shape[1]
    tm = _pick(M, (1536, 1024, 512, 384, 256, 128, 64))
    return pl.pallas_call(
        _mm_kernel,
        grid=(N // tn, M // tm),
        in_specs=[pl.BlockSpec((tm, K), lambda n, m: (m, 0)),
                  pl.BlockSpec((K, tn), lambda n, m: (0, n)),
                  pl.BlockSpec((1, tn), lambda n, m: (0, n))],
        out_specs=pl.BlockSpec((tm, tn), lambda n, m: (m, n)),
        out_shape=jax.ShapeDtypeStruct((M, N), out_dtype),
        compiler_params=_cparams(("parallel", "parallel")),
        name="in_proj",
    )(x, w, scale)


def _ret_kernel(q_ref, k_ref, v_ref, rg_ref, gr_ref, dm_ref, cd_ref, kd_ref, gc_ref, gn_ref, br_ref,
                s0_ref, y_ref, sn_ref, s_scr, *, n_chunks):
    c = pl.program_id(1)

    @pl.when(c == 0)
    def _():
        s_scr[...] = s0_ref[0]

    for h in range(R_HEADS):
        q = q_ref[:, h * R_DK:(h + 1) * R_DK]
        k = k_ref[:, h * R_DK:(h + 1) * R_DK]
        v = v_ref[:, h * R_DV:(h + 1) * R_DV]
        s_old = s_scr[h]
        sc = lax.dot_general(q, k, (((1,), (1,)), ((), ())), preferred_element_type=f32) * dm_ref[h]
        inner = jnp.dot(sc.astype(bf16), v, preferred_element_type=f32)
        cross = jnp.dot(q, s_old.astype(bf16), preferred_element_type=f32) * cd_ref[h]
        o = inner + cross
        kdk = (k.astype(f32) * kd_ref[h]).astype(bf16)
        upd = lax.dot_general(kdk, v, (((0,), (0,)), ((), ())), preferred_element_type=f32)
        s_scr[h] = gc_ref[h] * s_old + upd
        mu = jnp.mean(o, axis=-1, keepdims=True)
        d = o - mu
        var = jnp.mean(d * d, axis=-1, keepdims=True)
        cols = slice(h * R_DV, (h + 1) * R_DV)
        on = d * lax.rsqrt(var + LN_EPS) * gn_ref[:, cols]
        rg = rg_ref[:, cols]
        gate = _sigmoid(gr_ref[:, cols] + br_ref[:, cols])
        y_ref[:, cols] = gate * (rg * _sigmoid(rg) * on)

    @pl.when(c == n_chunks - 1)
    def _():
        sn_ref[0] = s_scr[...]


def _retention_tables(C):
    lg = np.log1p(-np.exp2(-5.0 - np.arange(R_HEADS, dtype=np.float64)))
    i = np.arange(C, dtype=np.float64)
    diff = i[:, None] - i[None, :]
    dm = np.where(diff >= 0, np.exp(lg[:, None, None] * np.maximum(diff, 0.0)), 0.0)
    cd = np.exp((i + 1.0)[None, :] * lg[:, None])
    kd = np.exp((C - 1.0 - i)[None, :] * lg[:, None])
    gc = np.exp(C * lg)
    return (jnp.asarray(dm, f32),
            jnp.asarray(np.broadcast_to(cd[:, :, None], (R_HEADS, C, R_DV)), f32),
            jnp.asarray(np.broadcast_to(kd[:, :, None], (R_HEADS, C, R_DK)), f32),
            jnp.asarray(np.broadcast_to(gc[:, None, None], (R_HEADS, 1, R_DV)), f32))


def _retention(g1, g2, gn_g, b_r, s0, row0, B, T):
    C = min(T, 256)
    n_chunks = T // C
    rb0 = row0 // C
    dm, cd, kd, gc = _retention_tables(C)
    rows = lambda b, c: rb0 + b * n_chunks + c
    const3 = lambda b, c: (0, 0, 0)
    return pl.pallas_call(
        functools.partial(_ret_kernel, n_chunks=n_chunks),
        grid=(B, n_chunks),
        in_specs=[pl.BlockSpec((C, 1024), lambda b, c: (rows(b, c), 0)),
                  pl.BlockSpec((C, 1024), lambda b, c: (rows(b, c), 1)),
                  pl.BlockSpec((C, 2048), lambda b, c: (rows(b, c), 1)),
                  pl.BlockSpec((C, 2048), lambda b, c: (rows(b, c), 0)),
                  pl.BlockSpec((C, 2048), lambda b, c: (rows(b, c), 1)),
                  pl.BlockSpec((R_HEADS, C, C), const3),
                  pl.BlockSpec((R_HEADS, C, R_DV), const3),
                  pl.BlockSpec((R_HEADS, C, R_DK), const3),
                  pl.BlockSpec((R_HEADS, 1, R_DV), const3),
                  pl.BlockSpec((1, D_MODEL), lambda b, c: (0, 0)),
                  pl.BlockSpec((1, D_MODEL), lambda b, c: (0, 0)),
                  pl.BlockSpec((1, R_HEADS, R_DK, R_DV), lambda b, c: (b, 0, 0, 0))],
        out_specs=[pl.BlockSpec((C, D_MODEL), lambda b, c: (b * n_chunks + c, 0)),
                   pl.BlockSpec((1, R_HEADS, R_DK, R_DV), lambda b, c: (b, 0, 0, 0))],
        out_shape=[jax.ShapeDtypeStruct((B * T, D_MODEL), f32),
                   jax.ShapeDtypeStruct((B, R_HEADS, R_DK, R_DV), f32)],
        scratch_shapes=[pltpu.VMEM((R_HEADS, R_DK, R_DV), f32)],
        compiler_params=_cparams(("parallel", "arbitrary")),
        name="retention",
    )(g1, g1, g1, g2, g2, dm, cd, kd, gc, gn_g, b_r, s0)


def _key_to_float(u):
    key = u ^ jnp.int32(INT_MIN)
    bits = key ^ ((key >> 31) & jnp.int32(0x7FFFFFFF))
    return lax.bitcast_convert_type(bits, f32)


def _dsa_kernel(aq_ref, iq_ref, ikw_ref, kia_ref, k_ref, v_ref, yr_ref, ga_ref, ba_ref, o_ref,
                qi_scr, qg_scr, isc_scr, thr_scr, m_scr, l_scr, acc_scr,
                *, TQ, TK, L, P, n_sel, nkb_total):
    i = pl.program_id(1)
    kb = pl.program_id(2)
    q0 = i * TQ
    lim_max = jnp.minimum(((P + q0 + TQ - 1) // CHUNK + 1) * CHUNK, L)
    nkb = (lim_max + TK - 1) // TK
    pos = P + q0 + lax.broadcasted_iota(jnp.int32, (TQ, 1), 0)
    lim = jnp.minimum((lax.shift_right_logical(pos, 6) + 1) * CHUNK, L)

    @pl.when(kb == 0)
    def _prepare():
        for h in range(I_HEADS):
            qi_scr[h * TQ:(h + 1) * TQ, :] = iq_ref[:, h * I_DH:(h + 1) * I_DH]
        for g in range(A_KV_HEADS):
            for hg in range(A_GROUP):
                h = g * A_GROUP + hg
                qg_scr[g, hg * TQ:(hg + 1) * TQ, :] = aq_ref[:, h * A_DH:(h + 1) * A_DH]
        w = ikw_ref[:, I_DH:I_DH + I_HEADS] * (I_DH ** -0.5 * I_HEADS ** -0.5)

        def score_block(kbb, carry):
            start = pl.multiple_of(kbb * TK, TK)
            ki = kia_ref[pl.ds(start, TK), :]
            logits = lax.dot_general(qi_scr[...], ki, (((1,), (1,)), ((), ())),
                                     preferred_element_type=f32)
            isc = jnp.zeros((TQ, TK), f32)
            for h in range(I_HEADS):
                isc = isc + jnp.maximum(logits[h * TQ:(h + 1) * TQ], 0.0) * w[:, h:h + 1]
            j = start + lax.broadcasted_iota(jnp.int32, (TQ, TK), 1)
            isc_scr[kbb] = jnp.where(j < lim, isc, -jnp.inf)
            return carry

        lax.fori_loop(0, nkb, score_block, 0)

        def bit_pass(b, u):
            cand = u | lax.shift_left(jnp.int32(1), 31 - b)
            t = _key_to_float(cand)

            def count_block(kbb, cnt):
                x = isc_scr[kbb]
                for c in range(TK // LANES):
                    cnt = cnt + jnp.where(x[:, c * LANES:(c + 1) * LANES] >= t, 1.0, 0.0)
                return cnt

            cnt = lax.fori_loop(0, nkb, count_block, jnp.zeros((TQ, LANES), f32))
            total = jnp.sum(cnt, axis=1, keepdims=True)
            return jnp.where(total >= float(n_sel), cand, u)

        u = lax.fori_loop(0, 32, bit_pass, jnp.zeros((TQ, 1), jnp.int32))
        thr = jnp.where(lim <= n_sel, F32_LOWEST, _key_to_float(u))
        thr_scr[...] = jnp.broadcast_to(thr, (TQ, LANES))
        m_scr[...] = jnp.full(m_scr.shape, NEG_BIG, f32)
        l_scr[...] = jnp.zeros(l_scr.shape, f32)
        acc_scr[...] = jnp.zeros(acc_scr.shape, f32)

    @pl.when(kb < nkb)
    def _attend():
        x = isc_scr[kb]
        sel = x >= thr_scr[:, 0:1]
        j = kb * TK + lax.broadcasted_iota(jnp.int32, (TQ, TK), 1)
        dist = jnp.abs(pos - j).astype(f32)
        for g in range(A_KV_HEADS):
            kg = k_ref[:, g * A_DH:(g + 1) * A_DH]
            vg = v_ref[:, g * A_DH:(g + 1) * A_DH]
            s = lax.dot_general(qg_scr[g], kg, (((1,), (1,)), ((), ())),
                                preferred_element_type=f32)
            parts = []
            for hg in range(A_GROUP):
                slope = 2.0 ** (-8.0 * (g * A_GROUP + hg + 1) / A_HEADS)
                sh = s[hg * TQ:(hg + 1) * TQ] - slope * dist
                parts.append(jnp.where(sel, sh, NEG_BIG))
            s = jnp.concatenate(parts, axis=0)
            m_prev = m_scr[g]
            m_new = jnp.maximum(m_prev, jnp.max(s, axis=1, keepdims=True))
            p = jnp.exp(s - m_new[:, 0:1])
            a = jnp.exp(m_prev - m_new)
            l_scr[g] = a * l_scr[g] + jnp.sum(p, axis=1, keepdims=True)
            acc_scr[g] = a * acc_scr[g] + jnp.dot(p.astype(bf16), vg, preferred_element_type=f32)
            m_scr[g] = m_new

    @pl.when(kb == nkb_total - 1)
    def _finish():
        gate = _sigmoid(ga_ref[...] + ba_ref[...])
        for g in range(A_KV_HEADS):
            og = acc_scr[g] / l_scr[g]
            for hg in range(A_GROUP):
                cols = slice((g * A_GROUP + hg) * A_DH, (g * A_GROUP + hg + 1) * A_DH)
                ya = og[hg * TQ:(hg + 1) * TQ]
                o_ref[:, cols] = (yr_ref[:, cols] + gate[:, cols] * ya).astype(o_ref.dtype)


def _dsa(g1, g2, ikw, kia, k_all, v_all, yr, b_a, row0, B, T, P, L, TQ, TK):
    Lp = kia.shape[1]
    nkb_total = Lp // TK
    nq = T // TQ
    rb0 = row0 // TQ
    n_sel = min(TOPK_MAX, L // 4)
    rows = lambda b, i: rb0 + b * nq + i

    def kv_map(b, i, kb):
        lim_max = jnp.minimum(((P + i * TQ + TQ - 1) // CHUNK + 1) * CHUNK, L)
        return (b, jnp.minimum(kb, (lim_max + TK - 1) // TK - 1), 0)

    kern = functools.partial(_dsa_kernel, TQ=TQ, TK=TK, L=L, P=P, n_sel=n_sel, nkb_total=nkb_total)
    return pl.pallas_call(
        kern,
        grid=(B, nq, nkb_total),
        in_specs=[pl.BlockSpec((TQ, 2048), lambda b, i, kb: (rows(b, i), 2)),
                  pl.BlockSpec((TQ, 1024), lambda b, i, kb: (rows(b, i), 6)),
                  pl.BlockSpec((TQ, LANES), lambda b, i, kb: (rows(b, i), 0)),
                  pl.BlockSpec((None, Lp, I_DH), lambda b, i, kb: (b, 0, 0)),
                  pl.BlockSpec((None, TK, 512), kv_map),
                  pl.BlockSpec((None, TK, 512), kv_map),
                  pl.BlockSpec((TQ, D_MODEL), lambda b, i, kb: (b * nq + i, 0)),
                  pl.BlockSpec((TQ, 2048), lambda b, i, kb: (rows(b, i), 2)),
                  pl.BlockSpec((1, D_MODEL), lambda b, i, kb: (0, 0))],
        out_specs=pl.BlockSpec((TQ, D_MODEL), lambda b, i, kb: (b * nq + i, 0)),
        out_shape=jax.ShapeDtypeStruct((B * T, D_MODEL), bf16),
        scratch_shapes=[pltpu.VMEM((I_HEADS * TQ, I_DH), bf16),
                        pltpu.VMEM((A_KV_HEADS, A_GROUP * TQ, A_DH), bf16),
                        pltpu.VMEM((nkb_total, TQ, TK), f32),
                        pltpu.VMEM((TQ, LANES), f32),
                        pltpu.VMEM((A_KV_HEADS, A_GROUP * TQ, LANES), f32),
                        pltpu.VMEM((A_KV_HEADS, A_GROUP * TQ, LANES), f32),
                        pltpu.VMEM((A_KV_HEADS, A_GROUP * TQ, A_DH), f32)],
        compiler_params=_cparams(("parallel", "arbitrary", "arbitrary")),
        name="dsa",
    )(g1, g1, ikw, kia, k_all, v_all, yr, g2, b_a)


def _mm_ln_kernel(x_ref, w_ref, r_ref, g_ref, b_ref, o_ref, ob_ref, acc_scr, *, nk):
    kk = pl.program_id(1)

    @pl.when(kk == 0)
    def _():
        acc_scr[...] = jnp.zeros(acc_scr.shape, f32)

    acc_scr[...] += jnp.dot(x_ref[...], w_ref[...], preferred_element_type=f32)

    @pl.when(kk == nk - 1)
    def _():
        y = ALPHA * r_ref[...] + acc_scr[...]
        mu = jnp.mean(y, axis=-1, keepdims=True)
        d = y - mu
        var = jnp.mean(d * d, axis=-1, keepdims=True)
        out = d * lax.rsqrt(var + LN_EPS) * g_ref[...] + b_ref[...]
        o_ref[...] = out
        ob_ref[...] = out.astype(bf16)


def _matmul_ln(x, w, resid, gamma, beta):
    M, K = x.shape
    N = w.shape[1]
    tm = _pick(M, (512, 384, 256, 128, 64))
    tk = _pick(K, (512, 256, 128))
    nk = K // tk
    return pl.pallas_call(
        functools.partial(_mm_ln_kernel, nk=nk),
        grid=(M // tm, nk),
        in_specs=[pl.BlockSpec((tm, tk), lambda m, k: (m, k)),
                  pl.BlockSpec((tk, N), lambda m, k: (k, 0)),
                  pl.BlockSpec((tm, N), lambda m, k: (m, 0)),
                  pl.BlockSpec((1, N), lambda m, k: (0, 0)),
                  pl.BlockSpec((1, N), lambda m, k: (0, 0))],
        out_specs=[pl.BlockSpec((tm, N), lambda m, k: (m, 0)),
                   pl.BlockSpec((tm, N), lambda m, k: (m, 0))],
        out_shape=[jax.ShapeDtypeStruct((M, N), f32), jax.ShapeDtypeStruct((M, N), bf16)],
        scratch_shapes=[pltpu.VMEM((tm, N), f32)],
        compiler_params=_cparams(("parallel", "arbitrary")),
        name="proj_ln",
    )(x, w, resid, gamma, beta)


def _gelu_tanh(x):
    return 0.5 * x * (1.0 + jnp.tanh(0.7978845608028654 * (x + 0.044715 * (x * x * x))))


def _up_kernel(x_ref, wh_ref, wu_ref, cw_ref, cb_ref, buf_ref, o_ref, nb_ref, carry_scr, *, tm, n_mt):
    m = pl.program_id(2)

    @pl.when(m == 0)
    def _():
        carry_scr[0:8 - (CONV_W - 1), :] = jnp.zeros((8 - (CONV_W - 1), carry_scr.shape[1]), f32)
        carry_scr[8 - (CONV_W - 1):8, :] = buf_ref[0]

    x = x_ref[...]
    h = jnp.dot(x, wh_ref[...], preferred_element_type=f32)
    u = jnp.dot(x, wu_ref[...], preferred_element_type=f32)
    prev = carry_scr[...]
    row = lax.broadcasted_iota(jnp.int32, h.shape, 0)
    hc = cb_ref[...] + cw_ref[CONV_W - 1:CONV_W, :] * h
    for s in range(1, CONV_W):
        shifted = pltpu.roll(h, s, 0)
        head = pltpu.roll(prev, s, 0)
        head = jnp.concatenate([head] * (tm // 8), axis=0)
        shifted = jnp.where(row < s, head, shifted)
        hc = hc + cw_ref[CONV_W - 1 - s:CONV_W - s, :] * shifted
    o_ref[...] = (_gelu_tanh(hc) * u).astype(o_ref.dtype)
    carry_scr[...] = h[tm - 8:tm, :]

    @pl.when(m == n_mt - 1)
    def _():
        nb_ref[0] = h[tm - (CONV_W - 1):tm, :]


def _up_conv(x, w_h, w_u, conv_w, conv_b, buf, row0, B, T):
    K = x.shape[1]
    tm = _pick(T, (512, 256, 128, 64))
    tn = 512
    n_mt = T // tm
    rb0 = row0 // tm
    return pl.pallas_call(
        functools.partial(_up_kernel, tm=tm, n_mt=n_mt),
        grid=(D_FF // tn, B, n_mt),
        in_specs=[pl.BlockSpec((tm, K), lambda n, b, m: (rb0 + b * n_mt + m, 0)),
                  pl.BlockSpec((K, tn), lambda n, b, m: (0, n)),
                  pl.BlockSpec((K, tn), lambda n, b, m: (0, n)),
                  pl.BlockSpec((CONV_W, tn), lambda n, b, m: (0, n)),
                  pl.BlockSpec((1, tn), lambda n, b, m: (0, n)),
                  pl.BlockSpec((1, CONV_W - 1, tn), lambda n, b, m: (b, 0, n))],
        out_specs=[pl.BlockSpec((tm, tn), lambda n, b, m: (b * n_mt + m, n)),
                   pl.BlockSpec((1, CONV_W - 1, tn), lambda n, b, m: (b, 0, n))],
        out_shape=[jax.ShapeDtypeStruct((B * T, D_FF), bf16),
                   jax.ShapeDtypeStruct((B, CONV_W - 1, D_FF), f32)],
        scratch_shapes=[pltpu.VMEM((8, tn), f32)],
        compiler_params=_cparams(("parallel", "arbitrary", "arbitrary")),
        name="up_conv",
    )(x, w_h, w_u, conv_w, conv_b, buf)


def _pad_keys(a, lp):
    return jnp.pad(a, ((0, 0), (0, lp - a.shape[1]), (0, 0)))


def kernel(x_prompt, x_sample, cache_attn_k, cache_attn_v, cache_idx_k, state_ret, state_conv,
           w_in, b_gate, gn_g, w_out, ln1_g, ln1_b, w_up, conv_w, conv_b, w_down, ln2_g, ln2_b):
    Bp, Tp, _ = x_prompt.shape
    Bs, Ts, _ = x_sample.shape
    P = cache_attn_k.shape[2]
    Mp, Ms = Bp * Tp, Bs * Ts
    M = Mp + Ms
    x_all = jnp.concatenate([x_prompt.reshape(Mp, D_MODEL), x_sample.reshape(Ms, D_MODEL)], axis=0)
    xb = x_all.astype(bf16)

    splits = (R_HEADS * R_DK, R_HEADS * R_DK, D_MODEL, D_MODEL, A_HEADS * A_DH, A_KV_HEADS * A_DH,
              A_KV_HEADS * A_DH, I_HEADS * I_DH, I_DH, I_HEADS, D_MODEL, D_MODEL)
    off = np.concatenate([[0], np.cumsum(splits)])
    wi = w_in[0]
    col = lambda i: wi[:, off[i]:off[i + 1]]
    rq, rk, rv, rg, aq, ak, av, iq, ik, iw, g_r, g_a = range(12)
    w1 = jnp.concatenate([col(rq), col(rk), col(rv), col(aq), col(iq)], axis=1).astype(bf16)
    s1 = jnp.concatenate([jnp.ones((1024,), f32), jnp.full((1024,), R_DK ** -0.5, f32),
                          jnp.ones((2048,), f32), jnp.full((2048,), A_DH ** -0.5, f32),
                          jnp.ones((1024,), f32)])[None]
    w2 = jnp.concatenate([col(rg), col(g_r), col(g_a)], axis=1).astype(bf16)
    w3 = jnp.concatenate([col(ak), col(av)], axis=1).astype(bf16)
    w4 = jnp.pad(jnp.concatenate([col(ik), col(iw)], axis=1),
                 ((0, 0), (0, LANES - I_DH - I_HEADS))).astype(bf16)
    g1 = _matmul(xb, w1, s1, bf16, 1024)
    g2 = _matmul(xb, w2, jnp.ones((1, w2.shape[1]), f32), f32, 1024)
    kv = _matmul(xb, w3, jnp.ones((1, w3.shape[1]), f32), f32, 512)
    ikw = _matmul(xb, w4, jnp.ones((1, LANES), f32), f32, LANES)

    b_r = b_gate[0, :D_MODEL][None]
    b_a = b_gate[0, D_MODEL:][None]
    gn = gn_g[0][None]

    yr_p, sn_p = _retention(g1, g2, gn, b_r, jnp.zeros((Bp, R_HEADS, R_DK, R_DV), f32), 0, Bp, Tp)
    yr_s, sn_s = _retention(g1, g2, gn, b_r, state_ret[0].astype(f32), Mp, Bs, Ts)

    kvb = kv.astype(bf16)
    ikb = ikw[:, :I_DH].astype(bf16)
    tq_p = _pick(Tp, (128,))
    tk_p = _pick(Tp, (512, 256, 128))
    mix_p = _dsa(g1, g2, ikw, ikb[:Mp].reshape(Bp, Tp, I_DH), kvb[:Mp, :512].reshape(Bp, Tp, 512),
                 kvb[:Mp, 512:].reshape(Bp, Tp, 512), yr_p, b_a, 0, Bp, Tp, 0, Tp, tq_p, tk_p)
    Ls = P + Ts
    lp = -(-Ls // LANES) * LANES
    ki_s = jnp.concatenate([cache_idx_k[0].astype(bf16), ikb[Mp:].reshape(Bs, Ts, I_DH)], axis=1)
    k_s = jnp.concatenate([cache_attn_k[0].reshape(Bs, P, 512).astype(bf16),
                           kvb[Mp:, :512].reshape(Bs, Ts, 512)], axis=1)
    v_s = jnp.concatenate([cache_attn_v[0].reshape(Bs, P, 512).astype(bf16),
                           kvb[Mp:, 512:].reshape(Bs, Ts, 512)], axis=1)
    mix_s = _dsa(g1, g2, ikw, _pad_keys(ki_s, lp), _pad_keys(k_s, lp), _pad_keys(v_s, lp), yr_s, b_a,
                 Mp, Bs, Ts, P, Ls, Ts, lp)
    mix = jnp.concatenate([mix_p, mix_s], axis=0)

    x1, x1b = _matmul_ln(mix, w_out[0].astype(bf16), x_all, ln1_g[0][None], ln1_b[0][None])
    wu = w_up[0].astype(bf16)
    w_h, w_u = wu[:, :D_FF], wu[:, D_FF:]
    ff_p, nb_p = _up_conv(x1b, w_h, w_u, conv_w[0], conv_b[0][None],
                          jnp.zeros((Bp, CONV_W - 1, D_FF), f32), 0, Bp, Tp)
    ff_s, nb_s = _up_conv(x1b, w_h, w_u, conv_w[0], conv_b[0][None], state_conv[0].astype(f32), Mp, Bs, Ts)
    ff = jnp.concatenate([ff_p, ff_s], axis=0)
    x2, _ = _matmul_ln(ff, w_down[0].astype(bf16), x1, ln2_g[0][None], ln2_b[0][None])

    dt = x_prompt.dtype
    return (x2[:Mp].reshape(Bp, Tp, D_MODEL).astype(dt),
            x2[Mp:].reshape(Bs, Ts, D_MODEL).astype(dt),
            kv[:Mp, :512].reshape(1, Bp, Tp, A_KV_HEADS, A_DH).astype(dt),
            kv[:Mp, 512:].reshape(1, Bp, Tp, A_KV_HEADS, A_DH).astype(dt),
            ikw[:Mp, :I_DH].reshape(1, Bp, Tp, I_DH).astype(dt),
            sn_p[None].astype(dt),
            nb_p[None].astype(dt),
            kv[Mp:, :512].reshape(1, Bs, Ts, A_KV_HEADS, A_DH).astype(dt),
            kv[Mp:, 512:].reshape(1, Bs, Ts, A_KV_HEADS, A_DH).astype(dt),
            ikw[Mp:, :I_DH].reshape(1, Bs, Ts, I_DH).astype(dt),
            sn_s[None].astype(dt),
            nb_s[None].astype(dt))
```

```python
import functools

import numpy as np
import jax
import jax.numpy as jnp
from jax import lax
from jax.experimental import pallas as pl
from jax.experimental.pallas import tpu as pltpu

f32 = jnp.float32
bf16 = jnp.bfloat16

D_MODEL = 2048
CHUNK = 64
R_HEADS = 8
R_DK = 128
R_DV = 256
A_HEADS = 16
A_KV_HEADS = 4
A_DH = 128
A_GROUP = 4
I_HEADS = 16
I_DH = 64
TOPK_MAX = 256
D_FF = 5632
CONV_W = 3
DEPTH = 1
ALPHA = (2 * DEPTH) ** 0.25
LN_EPS = 1e-5

LANES = 128
VMEM_LIMIT = 56 * 1024 * 1024
NEG_BIG = -1e30
F32_LOWEST = -3.4028234663852886e38
INT_MIN = -2 ** 31
LOG2E = 1.4426950408889634
JCUT_ALL = 2 ** 30


def _pick(n, cands):
    for c in cands:
        if n % c == 0:
            return c
    raise ValueError(f"no tile in {cands} divides {n}")


def _cparams(sem):
    return pltpu.CompilerParams(dimension_semantics=sem, vmem_limit_bytes=VMEM_LIMIT)


def _sigmoid(x):
    return 1.0 / (1.0 + jnp.exp(-x))


def _mm_kernel(x_ref, w_ref, s_ref, o_ref):
    acc = jnp.dot(x_ref[...], w_ref[...], preferred_element_type=f32)
    o_ref[...] = (acc * s_ref[...]).astype(o_ref.dtype)


def _matmul(x, w, scale, out_dtype, tn):
    M, K = x.shape
    N = w.shape[1]
    tm = _pick(M, (1536, 1024, 512, 384, 256, 128, 64))
    return pl.pallas_call(
        _mm_kernel,
        grid=(N // tn, M // tm),
        in_specs=[pl.BlockSpec((tm, K), lambda n, m: (m, 0)),
                  pl.BlockSpec((K, tn), lambda n, m: (0, n)),
                  pl.BlockSpec((1, tn), lambda n, m: (0, n))],
        out_specs=pl.BlockSpec((tm, tn), lambda n, m: (m, n)),
        out_shape=jax.ShapeDtypeStruct((M, N), out_dtype),
        compiler_params=_cparams(("parallel", "parallel")),
        name="in_proj",
    )(x, w, scale)


def _ret_kernel(q_ref, k_ref, v_ref, rg_ref, gr_ref, dm_ref, cd_ref, kd_ref, gc_ref, gn_ref, br_ref,
                s0_ref, y_ref, sn_ref, s_scr, *, n_chunks):
    c = pl.program_id(1)

    @pl.when(c == 0)
    def _():
        s_scr[...] = s0_ref[0]

    for h in range(R_HEADS):
        q = q_ref[:, h * R_DK:(h + 1) * R_DK]
        k = k_ref[:, h * R_DK:(h + 1) * R_DK]
        v = v_ref[:, h * R_DV:(h + 1) * R_DV]
        s_old = s_scr[h]
        sc = lax.dot_general(q, k, (((1,), (1,)), ((), ())), preferred_element_type=f32) * dm_ref[h]
        inner = jnp.dot(sc.astype(bf16), v, preferred_element_type=f32)
        cross = jnp.dot(q, s_old.astype(bf16), preferred_element_type=f32) * cd_ref[h]
        o = inner + cross
        kdk = (k.astype(f32) * kd_ref[h]).astype(bf16)
        upd = lax.dot_general(kdk, v, (((0,), (0,)), ((), ())), preferred_element_type=f32)
        s_scr[h] = gc_ref[h] * s_old + upd
        mu = jnp.mean(o, axis=-1, keepdims=True)
        d = o - mu
        var = jnp.mean(d * d, axis=-1, keepdims=True)
        cols = slice(h * R_DV, (h + 1) * R_DV)
        on = d * lax.rsqrt(var + LN_EPS) * gn_ref[:, cols]
        rg = rg_ref[:, cols]
        gate = _sigmoid(gr_ref[:, cols] + br_ref[:, cols])
        y_ref[:, cols] = gate * (rg * _sigmoid(rg) * on)

    @pl.when(c == n_chunks - 1)
    def _():
        sn_ref[0] = s_scr[...]


def _retention_tables(C):
    lg = np.log1p(-np.exp2(-5.0 - np.arange(R_HEADS, dtype=np.float64)))
    i = np.arange(C, dtype=np.float64)
    diff = i[:, None] - i[None, :]
    dm = np.where(diff >= 0, np.exp(lg[:, None, None] * np.maximum(diff, 0.0)), 0.0)
    cd = np.exp((i + 1.0)[None, :] * lg[:, None])
    kd = np.exp((C - 1.0 - i)[None, :] * lg[:, None])
    gc = np.exp(C * lg)
    return (jnp.asarray(dm, f32),
            jnp.asarray(np.broadcast_to(cd[:, :, None], (R_HEADS, C, R_DV)), f32),
            jnp.asarray(np.broadcast_to(kd[:, :, None], (R_HEADS, C, R_DK)), f32),
            jnp.asarray(np.broadcast_to(gc[:, None, None], (R_HEADS, 1, R_DV)), f32))


def _retention(g1, g2, gn_g, b_r, s0, row0, B, T):
    C = min(T, 256)
    n_chunks = T // C
    rb0 = row0 // C
    dm, cd, kd, gc = _retention_tables(C)
    rows = lambda b, c: rb0 + b * n_chunks + c
    const3 = lambda b, c: (0, 0, 0)
    return pl.pallas_call(
        functools.partial(_ret_kernel, n_chunks=n_chunks),
        grid=(B, n_chunks),
        in_specs=[pl.BlockSpec((C, 1024), lambda b, c: (rows(b, c), 0)),
                  pl.BlockSpec((C, 1024), lambda b, c: (rows(b, c), 1)),
                  pl.BlockSpec((C, 2048), lambda b, c: (rows(b, c), 1)),
                  pl.BlockSpec((C, 2048), lambda b, c: (rows(b, c), 0)),
                  pl.BlockSpec((C, 2048), lambda b, c: (rows(b, c), 1)),
                  pl.BlockSpec((R_HEADS, C, C), const3),
                  pl.BlockSpec((R_HEADS, C, R_DV), const3),
                  pl.BlockSpec((R_HEADS, C, R_DK), const3),
                  pl.BlockSpec((R_HEADS, 1, R_DV), const3),
                  pl.BlockSpec((1, D_MODEL), lambda b, c: (0, 0)),
                  pl.BlockSpec((1, D_MODEL), lambda b, c: (0, 0)),
                  pl.BlockSpec((1, R_HEADS, R_DK, R_DV), lambda b, c: (b, 0, 0, 0))],
        out_specs=[pl.BlockSpec((C, D_MODEL), lambda b, c: (b * n_chunks + c, 0)),
                   pl.BlockSpec((1, R_HEADS, R_DK, R_DV), lambda b, c: (b, 0, 0, 0))],
        out_shape=[jax.ShapeDtypeStruct((B * T, D_MODEL), f32),
                   jax.ShapeDtypeStruct((B, R_HEADS, R_DK, R_DV), f32)],
        scratch_shapes=[pltpu.VMEM((R_HEADS, R_DK, R_DV), f32)],
        compiler_params=_cparams(("parallel", "arbitrary")),
        name="retention",
    )(g1, g1, g1, g2, g2, dm, cd, kd, gc, gn_g, b_r, s0)


def _bf16_split3(x):
    parts = []
    rem = np.float64(x)
    for _ in range(3):
        p = np.float64(np.asarray(rem, np.float32).astype(jnp.bfloat16).astype(np.float32))
        parts.append(p)
        rem = rem - p
    return parts


def _alibi_tables(TK):
    r = 1
    while TK // r > 256:
        r *= 2
    jj = np.arange(TK)
    kx = np.zeros((TK, LANES), np.float32)
    kx[:, 0:3] = (jj // r)[:, None]
    kx[:, 3:6] = (jj % r)[:, None]
    sig = np.zeros((A_HEADS, LANES), np.float32)
    sig_eff = []
    for h in range(A_HEADS):
        s3 = _bf16_split3(2.0 ** (-8.0 * (h + 1) / A_HEADS) * LOG2E)
        sig[h, 0:3] = [r * p for p in s3]
        sig[h, 3:6] = s3
        sig_eff.append(float(sum(s3)))
    vx = np.zeros((TK, LANES), np.float32)
    vx[:, 0] = 1.0
    return jnp.asarray(kx, bf16), jnp.asarray(vx, bf16), jnp.asarray(sig, f32), tuple(sig_eff)


def _key_to_float(u):
    key = u ^ jnp.int32(INT_MIN)
    bits = key ^ ((key >> 31) & jnp.int32(0x7FFFFFFF))
    return lax.bitcast_convert_type(bits, f32)


def _dsa_kernel(si_ref, skb_ref, aq_ref, iq_ref, ikw_ref, kia_ref, k_ref, v_ref, kx_ref, vx_ref, sig_ref,
                yr_ref, ga_ref, ba_ref, o_ref,
                qi_scr, qg_scr, isc_scr, thr_scr, jcut_scr, m_scr, acc_scr,
                *, TQ, TK, TS, L, P, n_sel, sig_eff, idx_bits):
    t = pl.program_id(1)
    i = si_ref[t]
    kb = skb_ref[t]
    r = TK // TS
    q0 = P + i * TQ
    lim_max = jnp.minimum(((q0 + TQ - 1) // CHUNK + 1) * CHUNK, L)
    nkb = (lim_max + TK - 1) // TK
    nsb = (lim_max + TS - 1) // TS
    pos = q0 + lax.broadcasted_iota(jnp.int32, (TQ, 1), 0)
    lim = jnp.minimum((lax.shift_right_logical(pos, 6) + 1) * CHUNK, L)

    def count_rows(pred):
        lane = lax.broadcasted_iota(jnp.int32, (TQ, LANES), 1)

        def block(sb, cnt):
            x = isc_scr[sb]
            for c in range(TS // LANES):
                hit = pred(x[:, c * LANES:(c + 1) * LANES], sb * TS + c * LANES + lane)
                cnt = cnt + jnp.where(hit, 1.0, 0.0)
            return cnt

        cnt = lax.fori_loop(0, nsb, block, jnp.zeros((TQ, LANES), f32))
        return jnp.sum(cnt, axis=1, keepdims=True)

    @pl.when(kb == 0)
    def _prepare():
        for h in range(I_HEADS):
            qi_scr[h * TQ:(h + 1) * TQ, :] = iq_ref[:, h * I_DH:(h + 1) * I_DH]
        for g in range(A_KV_HEADS):
            for hg in range(A_GROUP):
                h = g * A_GROUP + hg
                qg_scr[g, hg * TQ:(hg + 1) * TQ, 0:A_DH] = aq_ref[:, h * A_DH:(h + 1) * A_DH]
                qg_scr[g, hg * TQ:(hg + 1) * TQ, A_DH:2 * A_DH] = jnp.broadcast_to(
                    sig_ref[h:h + 1, :], (TQ, LANES)).astype(bf16)
        w = ikw_ref[:, I_DH:I_DH + I_HEADS] * (I_DH ** -0.5 * I_HEADS ** -0.5)

        def score_block(sb, carry):
            start = pl.multiple_of(sb * TS, TS)
            ki = kia_ref[pl.ds(start, TS), :]
            logits = lax.dot_general(qi_scr[...], ki, (((1,), (1,)), ((), ())),
                                     preferred_element_type=f32)
            isc = jnp.zeros((TQ, TS), f32)
            for h in range(I_HEADS):
                isc = isc + jnp.maximum(logits[h * TQ:(h + 1) * TQ], 0.0) * w[:, h:h + 1]
            j = start + lax.broadcasted_iota(jnp.int32, (TQ, TS), 1)
            isc_scr[sb] = jnp.where(j < lim, isc, -jnp.inf)
            return carry

        lax.fori_loop(0, nsb, score_block, 0)

        def fill_block(sb, carry):
            isc_scr[sb] = jnp.full((TQ, TS), -jnp.inf, f32)
            return carry

        lax.fori_loop(nsb, nkb * r, fill_block, 0)

        def bit_pass(b, carry):
            u, cu = carry
            cand = u | lax.shift_left(jnp.int32(1), 31 - b)
            tf = _key_to_float(cand)
            total = count_rows(lambda x, j: x >= tf)
            ok = total >= float(n_sel)
            return jnp.where(ok, cand, u), jnp.where(ok, total, cu)

        u, cu = lax.fori_loop(0, 32, bit_pass, (jnp.zeros((TQ, 1), jnp.int32), jnp.zeros((TQ, 1), f32)))
        thr = jnp.where(lim <= n_sel, F32_LOWEST, _key_to_float(u))
        thr_scr[...] = jnp.broadcast_to(thr, (TQ, LANES))
        jcut_scr[...] = jnp.full((TQ, LANES), JCUT_ALL, jnp.int32)
        m_scr[...] = jnp.full(m_scr.shape, NEG_BIG, f32)
        acc_scr[...] = jnp.zeros(acc_scr.shape, f32)

        tied = jnp.logical_and(lim > n_sel, cu > float(n_sel))

        @pl.when(jnp.max(jnp.where(tied, 1.0, 0.0)) > 0.0)
        def _ties():
            room = float(n_sel) - count_rows(lambda x, j: x > thr)

            def idx_pass(b, jc):
                cand = jc | lax.shift_left(jnp.int32(1), idx_bits - 1 - b)
                c = count_rows(lambda x, j: jnp.logical_and(x == thr, j < cand))
                return jnp.where(c <= room, cand, jc)

            jc = lax.fori_loop(0, idx_bits, idx_pass, jnp.zeros((TQ, 1), jnp.int32))
            jcut_scr[...] = jnp.broadcast_to(jnp.where(tied, jc, JCUT_ALL), (TQ, LANES))

    def attend(diag):
        x = jnp.concatenate([isc_scr[kb * r + c] for c in range(r)], axis=1)
        thr = thr_scr[:, 0:1]
        j = kb * TK + lax.broadcasted_iota(jnp.int32, (TQ, TK), 1)
        sel = jnp.logical_or(x > thr, jnp.logical_and(x == thr, j < jcut_scr[:, 0:1]))
        if diag:
            ahead = jnp.maximum(j - pos, 0).astype(f32)
        back = jnp.full((TQ, 1), kb * TK - q0, jnp.int32).astype(f32)
        for g in range(A_KV_HEADS):
            kg = jnp.concatenate([k_ref[:, g * A_DH:(g + 1) * A_DH], kx_ref[...]], axis=1)
            vg = jnp.concatenate([v_ref[:, g * A_DH:(g + 1) * A_DH], vx_ref[...]], axis=1)
            s = lax.dot_general(qg_scr[g], kg, (((1,), (1,)), ((), ())),
                                preferred_element_type=f32)
            ps, alphas = [], []
            for hg in range(A_GROUP):
                h = g * A_GROUP + hg
                sh = s[hg * TQ:(hg + 1) * TQ]
                if diag:
                    sh = sh - (2.0 * sig_eff[h]) * ahead
                sh = jnp.where(sel, sh, NEG_BIG)
                d = back * sig_eff[h]
                m_prev = m_scr[h]
                m_new = jnp.maximum(m_prev, jnp.max(sh, axis=1, keepdims=True) + d)
                ps.append(jnp.exp2(sh - (m_new[:, 0:1] - d)))
                alphas.append(jnp.exp2(m_prev - m_new))
                m_scr[h] = m_new
            p = jnp.concatenate(ps, axis=0).astype(bf16)
            a = jnp.concatenate(alphas, axis=0)
            acc_scr[g] = (jnp.concatenate([a, a], axis=1) * acc_scr[g]
                          + jnp.dot(p, vg, preferred_element_type=f32))

    is_diag = kb * TK + TK - 1 > q0

    @pl.when(is_diag)
    def _():
        attend(True)

    @pl.when(jnp.logical_not(is_diag))
    def _():
        attend(False)

    @pl.when(kb == nkb - 1)
    def _finish():
        gate = _sigmoid(ga_ref[...] + ba_ref[...])
        for g in range(A_KV_HEADS):
            acc = acc_scr[g]
            og = acc[:, 0:A_DH] / acc[:, A_DH:A_DH + 1]
            for hg in range(A_GROUP):
                cols = slice((g * A_GROUP + hg) * A_DH, (g * A_GROUP + hg + 1) * A_DH)
                ya = og[hg * TQ:(hg + 1) * TQ]
                o_ref[:, cols] = (yr_ref[:, cols] + gate[:, cols] * ya).astype(o_ref.dtype)


def _dsa(g1, g2, ikw, kia, k_all, v_all, yr, b_a, row0, B, T, P, L, TQ, TK, TS):
    Lp = kia.shape[1]
    nkb_total = Lp // TK
    nq = T // TQ
    rb0 = row0 // TQ
    n_sel = min(TOPK_MAX, L // 4)
    steps = []
    for i in range(nq):
        lim_max = min(((P + i * TQ + TQ - 1) // CHUNK + 1) * CHUNK, L)
        steps += [(i, kb) for kb in range(-(-lim_max // TK))]
    step_i = jnp.asarray([s[0] for s in steps], jnp.int32)
    step_kb = jnp.asarray([s[1] for s in steps], jnp.int32)
    kx, vx, sig, sig_eff = _alibi_tables(TK)
    rows = lambda b, t, si: rb0 + b * nq + si[t]
    kern = functools.partial(_dsa_kernel, TQ=TQ, TK=TK, TS=TS, L=L, P=P, n_sel=n_sel, sig_eff=sig_eff,
                             idx_bits=int(Lp).bit_length())
    grid_spec = pltpu.PrefetchScalarGridSpec(
        num_scalar_prefetch=2,
        grid=(B, len(steps)),
        in_specs=[pl.BlockSpec((TQ, 2048), lambda b, t, si, sk: (rows(b, t, si), 2)),
                  pl.BlockSpec((TQ, 1024), lambda b, t, si, sk: (rows(b, t, si), 6)),
                  pl.BlockSpec((TQ, LANES), lambda b, t, si, sk: (rows(b, t, si), 0)),
                  pl.BlockSpec((None, Lp, I_DH), lambda b, t, si, sk: (b, 0, 0)),
                  pl.BlockSpec((None, TK, 512), lambda b, t, si, sk: (b, sk[t], 0)),
                  pl.BlockSpec((None, TK, 512), lambda b, t, si, sk: (b, sk[t], 0)),
                  pl.BlockSpec((TK, LANES), lambda b, t, si, sk: (0, 0)),
                  pl.BlockSpec((TK, LANES), lambda b, t, si, sk: (0, 0)),
                  pl.BlockSpec((A_HEADS, LANES), lambda b, t, si, sk: (0, 0)),
                  pl.BlockSpec((TQ, D_MODEL), lambda b, t, si, sk: (b * nq + si[t], 0)),
                  pl.BlockSpec((TQ, 2048), lambda b, t, si, sk: (rows(b, t, si), 2)),
                  pl.BlockSpec((1, D_MODEL), lambda b, t, si, sk: (0, 0))],
        out_specs=pl.BlockSpec((TQ, D_MODEL), lambda b, t, si, sk: (b * nq + si[t], 0)),
        scratch_shapes=[pltpu.VMEM((I_HEADS * TQ, I_DH), bf16),
                        pltpu.VMEM((A_KV_HEADS, A_GROUP * TQ, 2 * A_DH), bf16),
                        pltpu.VMEM((nkb_total * (TK // TS), TQ, TS), f32),
                        pltpu.VMEM((TQ, LANES), f32),
                        pltpu.VMEM((TQ, LANES), jnp.int32),
                        pltpu.VMEM((A_HEADS, TQ, LANES), f32),
                        pltpu.VMEM((A_KV_HEADS, A_GROUP * TQ, 2 * A_DH), f32)])
    return pl.pallas_call(
        kern,
        grid_spec=grid_spec,
        out_shape=jax.ShapeDtypeStruct((B * T, D_MODEL), bf16),
        compiler_params=_cparams(("parallel", "arbitrary")),
        name="dsa",
    )(step_i, step_kb, g1, g1, ikw, kia, k_all, v_all, kx, vx, sig, yr, g2, b_a)


def _mm_ln_kernel(x_ref, w_ref, r_ref, g_ref, b_ref, o_ref, ob_ref, acc_scr, *, nk):
    kk = pl.program_id(1)

    @pl.when(kk == 0)
    def _():
        acc_scr[...] = jnp.zeros(acc_scr.shape, f32)

    acc_scr[...] += jnp.dot(x_ref[...], w_ref[...], preferred_element_type=f32)

    @pl.when(kk == nk - 1)
    def _():
        y = ALPHA * r_ref[...] + acc_scr[...]
        mu = jnp.mean(y, axis=-1, keepdims=True)
        d = y - mu
        var = jnp.mean(d * d, axis=-1, keepdims=True)
        out = d * lax.rsqrt(var + LN_EPS) * g_ref[...] + b_ref[...]
        o_ref[...] = out
        ob_ref[...] = out.astype(bf16)


def _matmul_ln(x, w, resid, gamma, beta):
    M, K = x.shape
    N = w.shape[1]
    tm = _pick(M, (512, 384, 256, 128, 64))
    tk = _pick(K, (2048, 1408, 512, 256, 128))
    nk = K // tk
    return pl.pallas_call(
        functools.partial(_mm_ln_kernel, nk=nk),
        grid=(M // tm, nk),
        in_specs=[pl.BlockSpec((tm, tk), lambda m, k: (m, k)),
                  pl.BlockSpec((tk, N), lambda m, k: (k, 0)),
                  pl.BlockSpec((tm, N), lambda m, k: (m, 0)),
                  pl.BlockSpec((1, N), lambda m, k: (0, 0)),
                  pl.BlockSpec((1, N), lambda m, k: (0, 0))],
        out_specs=[pl.BlockSpec((tm, N), lambda m, k: (m, 0)),
                   pl.BlockSpec((tm, N), lambda m, k: (m, 0))],
        out_shape=[jax.ShapeDtypeStruct((M, N), f32), jax.ShapeDtypeStruct((M, N), bf16)],
        scratch_shapes=[pltpu.VMEM((tm, N), f32)],
        compiler_params=_cparams(("parallel", "arbitrary")),
        name="proj_ln",
    )(x, w, resid, gamma, beta)


def _gelu_tanh(x):
    return 0.5 * x * (1.0 + jnp.tanh(0.7978845608028654 * (x + 0.044715 * (x * x * x))))


def _up_kernel(x_ref, wh_ref, wu_ref, cw_ref, cb_ref, buf_ref, o_ref, nb_ref, carry_scr, *, tm, n_mt):
    m = pl.program_id(2)

    @pl.when(m == 0)
    def _():
        carry_scr[0:8 - (CONV_W - 1), :] = jnp.zeros((8 - (CONV_W - 1), carry_scr.shape[1]), f32)
        carry_scr[8 - (CONV_W - 1):8, :] = buf_ref[0]

    x = x_ref[...]
    h = jnp.dot(x, wh_ref[...], preferred_element_type=f32)
    u = jnp.dot(x, wu_ref[...], preferred_element_type=f32)
    prev = carry_scr[...]
    row = lax.broadcasted_iota(jnp.int32, h.shape, 0)
    hc = cb_ref[...] + cw_ref[CONV_W - 1:CONV_W, :] * h
    for s in range(1, CONV_W):
        shifted = pltpu.roll(h, s, 0)
        head = pltpu.roll(prev, s, 0)
        head = jnp.concatenate([head] * (tm // 8), axis=0)
        shifted = jnp.where(row < s, head, shifted)
        hc = hc + cw_ref[CONV_W - 1 - s:CONV_W - s, :] * shifted
    o_ref[...] = (_gelu_tanh(hc) * u).astype(o_ref.dtype)
    carry_scr[...] = h[tm - 8:tm, :]

    @pl.when(m == n_mt - 1)
    def _():
        nb_ref[0] = h[tm - (CONV_W - 1):tm, :]


def _up_conv(x, w_h, w_u, conv_w, conv_b, buf, row0, B, T):
    K = x.shape[1]
    tm = _pick(T, (512, 256, 128, 64))
    tn = 512
    n_mt = T // tm
    rb0 = row0 // tm
    return pl.pallas_call(
        functools.partial(_up_kernel, tm=tm, n_mt=n_mt),
        grid=(D_FF // tn, B, n_mt),
        in_specs=[pl.BlockSpec((tm, K), lambda n, b, m: (rb0 + b * n_mt + m, 0)),
                  pl.BlockSpec((K, tn), lambda n, b, m: (0, n)),
                  pl.BlockSpec((K, tn), lambda n, b, m: (0, n)),
                  pl.BlockSpec((CONV_W, tn), lambda n, b, m: (0, n)),
                  pl.BlockSpec((1, tn), lambda n, b, m: (0, n)),
                  pl.BlockSpec((1, CONV_W - 1, tn), lambda n, b, m: (b, 0, n))],
        out_specs=[pl.BlockSpec((tm, tn), lambda n, b, m: (b * n_mt + m, n)),
                   pl.BlockSpec((1, CONV_W - 1, tn), lambda n, b, m: (b, 0, n))],
        out_shape=[jax.ShapeDtypeStruct((B * T, D_FF), bf16),
                   jax.ShapeDtypeStruct((B, CONV_W - 1, D_FF), f32)],
        scratch_shapes=[pltpu.VMEM((8, tn), f32)],
        compiler_params=_cparams(("parallel", "arbitrary", "arbitrary")),
        name="up_conv",
    )(x, w_h, w_u, conv_w, conv_b, buf)


def _pad_keys(a, lp):
    return jnp.pad(a, ((0, 0), (0, lp - a.shape[1]), (0, 0)))


def kernel(x_prompt, x_sample, cache_attn_k, cache_attn_v, cache_idx_k, state_ret, state_conv,
           w_in, b_gate, gn_g, w_out, ln1_g, ln1_b, w_up, conv_w, conv_b, w_down, ln2_g, ln2_b):
    Bp, Tp, _ = x_prompt.shape
    Bs, Ts, _ = x_sample.shape
    P = cache_attn_k.shape[2]
    Mp, Ms = Bp * Tp, Bs * Ts
    M = Mp + Ms
    x_all = jnp.concatenate([x_prompt.reshape(Mp, D_MODEL), x_sample.reshape(Ms, D_MODEL)], axis=0)
    xb = x_all.astype(bf16)

    splits = (R_HEADS * R_DK, R_HEADS * R_DK, D_MODEL, D_MODEL, A_HEADS * A_DH, A_KV_HEADS * A_DH,
              A_KV_HEADS * A_DH, I_HEADS * I_DH, I_DH, I_HEADS, D_MODEL, D_MODEL)
    off = np.concatenate([[0], np.cumsum(splits)])
    wi = w_in[0]
    col = lambda i: wi[:, off[i]:off[i + 1]]
    rq, rk, rv, rg, aq, ak, av, iq, ik, iw, g_r, g_a = range(12)
    w1 = jnp.concatenate([col(rq), col(rk), col(rv), col(aq), col(iq)], axis=1).astype(bf16)
    s1 = jnp.concatenate([jnp.ones((1024,), f32), jnp.full((1024,), R_DK ** -0.5, f32),
                          jnp.ones((2048,), f32), jnp.full((2048,), A_DH ** -0.5 * LOG2E, f32),
                          jnp.ones((1024,), f32)])[None]
    w2 = jnp.concatenate([col(rg), col(g_r), col(g_a)], axis=1).astype(bf16)
    w3 = jnp.concatenate([col(ak), col(av)], axis=1).astype(bf16)
    w4 = jnp.pad(jnp.concatenate([col(ik), col(iw)], axis=1),
                 ((0, 0), (0, LANES - I_DH - I_HEADS))).astype(bf16)
    g1 = _matmul(xb, w1, s1, bf16, 1024)
    g2 = _matmul(xb, w2, jnp.ones((1, w2.shape[1]), f32), f32, 1024)
    kv = _matmul(xb, w3, jnp.ones((1, w3.shape[1]), f32), f32, 512)
    ikw = _matmul(xb, w4, jnp.ones((1, LANES), f32), f32, LANES)

    b_r = b_gate[0, :D_MODEL][None]
    b_a = b_gate[0, D_MODEL:][None]
    gn = gn_g[0][None]

    yr_p, sn_p = _retention(g1, g2, gn, b_r, jnp.zeros((Bp, R_HEADS, R_DK, R_DV), f32), 0, Bp, Tp)
    yr_s, sn_s = _retention(g1, g2, gn, b_r, state_ret[0].astype(f32), Mp, Bs, Ts)

    kvb = kv.astype(bf16)
    ikb = ikw[:, :I_DH].astype(bf16)
    tq_p = _pick(Tp, (128,))
    tk_p = _pick(Tp, (1024, 512, 256, 128))
    ts_p = min(tk_p, 512)
    mix_p = _dsa(g1, g2, ikw, ikb[:Mp].reshape(Bp, Tp, I_DH), kvb[:Mp, :512].reshape(Bp, Tp, 512),
                 kvb[:Mp, 512:].reshape(Bp, Tp, 512), yr_p, b_a, 0, Bp, Tp, 0, Tp, tq_p, tk_p, ts_p)
    Ls = P + Ts
    lp = -(-Ls // LANES) * LANES
    ki_s = jnp.concatenate([cache_idx_k[0].astype(bf16), ikb[Mp:].reshape(Bs, Ts, I_DH)], axis=1)
    k_s = jnp.concatenate([cache_attn_k[0].reshape(Bs, P, 512).astype(bf16),
                           kvb[Mp:, :512].reshape(Bs, Ts, 512)], axis=1)
    v_s = jnp.concatenate([cache_attn_v[0].reshape(Bs, P, 512).astype(bf16),
                           kvb[Mp:, 512:].reshape(Bs, Ts, 512)], axis=1)
    mix_s = _dsa(g1, g2, ikw, _pad_keys(ki_s, lp), _pad_keys(k_s, lp), _pad_keys(v_s, lp), yr_s, b_a,
                 Mp, Bs, Ts, P, Ls, Ts, lp, lp)
    mix = jnp.concatenate([mix_p, mix_s], axis=0)

    x1, x1b = _matmul_ln(mix, w_out[0].astype(bf16), x_all, ln1_g[0][None], ln1_b[0][None])
    wu = w_up[0].astype(bf16)
    w_h, w_u = wu[:, :D_FF], wu[:, D_FF:]
    ff_p, nb_p = _up_conv(x1b, w_h, w_u, conv_w[0], conv_b[0][None],
                          jnp.zeros((Bp, CONV_W - 1, D_FF), f32), 0, Bp, Tp)
    ff_s, nb_s = _up_conv(x1b, w_h, w_u, conv_w[0], conv_b[0][None], state_conv[0].astype(f32), Mp, Bs, Ts)
    ff = jnp.concatenate([ff_p, ff_s], axis=0)
    x2, _ = _matmul_ln(ff, w_down[0].astype(bf16), x1, ln2_g[0][None], ln2_b[0][None])

    dt = x_prompt.dtype
    return (x2[:Mp].reshape(Bp, Tp, D_MODEL).astype(dt),
            x2[Mp:].reshape(Bs, Ts, D_MODEL).astype(dt),
            kv[:Mp, :512].reshape(1, Bp, Tp, A_KV_HEADS, A_DH).astype(dt),
            kv[:Mp, 512:].reshape(1, Bp, Tp, A_KV_HEADS, A_DH).astype(dt),
            ikw[:Mp, :I_DH].reshape(1, Bp, Tp, I_DH).astype(dt),
            sn_p[None].astype(dt),
            nb_p[None].astype(dt),
            kv[Mp:, :512].reshape(1, Bs, Ts, A_KV_HEADS, A_DH).astype(dt),
            kv[Mp:, 512:].reshape(1, Bs, Ts, A_KV_HEADS, A_DH).astype(dt),
            ikw[Mp:, :I_DH].reshape(1, Bs, Ts, I_DH).astype(dt),
            sn_s[None].astype(dt),
            nb_s[None].astype(dt))
```

```python
import functools

import numpy as np
import jax
import jax.numpy as jnp
from jax import lax
from jax.experimental import pallas as pl
from jax.experimental.pallas import tpu as pltpu

f32 = jnp.float32
bf16 = jnp.bfloat16

D_MODEL = 2048
CHUNK = 64
R_HEADS = 8
R_DK = 128
R_DV = 256
A_HEADS = 16
A_KV_HEADS = 4
A_DH = 128
A_GROUP = 4
I_HEADS = 16
I_DH = 64
TOPK_MAX = 256
D_FF = 5632
CONV_W = 3
DEPTH = 1
ALPHA = (2 * DEPTH) ** 0.25
LN_EPS = 1e-5

LANES = 128
VMEM_LIMIT = 56 * 1024 * 1024
NEG_BIG = -1e30
F32_LOWEST = -3.4028234663852886e38
INT_MIN = -2 ** 31
LOG2E = 1.4426950408889634
JCUT_ALL = 2 ** 30
UNDERFLOW_LOG2 = 160.0


def _pick(n, cands):
    for c in cands:
        if n % c == 0:
            return c
    raise ValueError(f"no tile in {cands} divides {n}")


def _cparams(sem):
    return pltpu.CompilerParams(dimension_semantics=sem, vmem_limit_bytes=VMEM_LIMIT)


def _sigmoid(x):
    return 1.0 / (1.0 + jnp.exp(-x))


def _mm_kernel(x_ref, w_ref, s_ref, o_ref):
    acc = jnp.dot(x_ref[...], w_ref[...], preferred_element_type=f32)
    o_ref[...] = (acc * s_ref[...]).astype(o_ref.dtype)


def _matmul(x, w, scale, out_dtype, tn):
    M, K = x.shape
    N = w.shape[1]
    tm = _pick(M, (1536, 1024, 512, 384, 256, 128, 64))
    return pl.pallas_call(
        _mm_kernel,
        grid=(N // tn, M // tm),
        in_specs=[pl.BlockSpec((tm, K), lambda n, m: (m, 0)),
                  pl.BlockSpec((K, tn), lambda n, m: (0, n)),
                  pl.BlockSpec((1, tn), lambda n, m: (0, n))],
        out_specs=pl.BlockSpec((tm, tn), lambda n, m: (m, n)),
        out_shape=jax.ShapeDtypeStruct((M, N), out_dtype),
        compiler_params=_cparams(("parallel", "parallel")),
        name="in_proj",
    )(x, w, scale)


def _ret_kernel(q_ref, k_ref, v_ref, rg_ref, gr_ref, dm_ref, cd_ref, kd_ref, gc_ref, gn_ref, br_ref,
                s0_ref, y_ref, sn_ref, s_scr, *, n_chunks):
    c = pl.program_id(1)

    @pl.when(c == 0)
    def _():
        s_scr[...] = s0_ref[0]

    for h in range(R_HEADS):
        q = q_ref[:, h * R_DK:(h + 1) * R_DK]
        k = k_ref[:, h * R_DK:(h + 1) * R_DK]
        v = v_ref[:, h * R_DV:(h + 1) * R_DV]
        s_old = s_scr[h]
        sc = lax.dot_general(q, k, (((1,), (1,)), ((), ())), preferred_element_type=f32) * dm_ref[h]
        inner = jnp.dot(sc.astype(bf16), v, preferred_element_type=f32)
        cross = jnp.dot(q, s_old.astype(bf16), preferred_element_type=f32) * cd_ref[h]
        o = inner + cross
        kdk = (k.astype(f32) * kd_ref[h]).astype(bf16)
        upd = lax.dot_general(kdk, v, (((0,), (0,)), ((), ())), preferred_element_type=f32)
        s_scr[h] = gc_ref[h] * s_old + upd
        mu = jnp.mean(o, axis=-1, keepdims=True)
        d = o - mu
        var = jnp.mean(d * d, axis=-1, keepdims=True)
        cols = slice(h * R_DV, (h + 1) * R_DV)
        on = d * lax.rsqrt(var + LN_EPS) * gn_ref[:, cols]
        rg = rg_ref[:, cols]
        gate = _sigmoid(gr_ref[:, cols] + br_ref[:, cols])
        y_ref[:, cols] = gate * (rg * _sigmoid(rg) * on)

    @pl.when(c == n_chunks - 1)
    def _():
        sn_ref[0] = s_scr[...]


def _retention_tables(C):
    lg = np.log1p(-np.exp2(-5.0 - np.arange(R_HEADS, dtype=np.float64)))
    i = np.arange(C, dtype=np.float64)
    diff = i[:, None] - i[None, :]
    dm = np.where(diff >= 0, np.exp(lg[:, None, None] * np.maximum(diff, 0.0)), 0.0)
    cd = np.exp((i + 1.0)[None, :] * lg[:, None])
    kd = np.exp((C - 1.0 - i)[None, :] * lg[:, None])
    gc = np.exp(C * lg)
    return (jnp.asarray(dm, f32),
            jnp.asarray(np.broadcast_to(cd[:, :, None], (R_HEADS, C, R_DV)), f32),
            jnp.asarray(np.broadcast_to(kd[:, :, None], (R_HEADS, C, R_DK)), f32),
            jnp.asarray(np.broadcast_to(gc[:, None, None], (R_HEADS, 1, R_DV)), f32))


def _retention(g1, g2, gn_g, b_r, s0, row0, B, T):
    C = min(T, 256)
    n_chunks = T // C
    rb0 = row0 // C
    dm, cd, kd, gc = _retention_tables(C)
    rows = lambda b, c: rb0 + b * n_chunks + c
    const3 = lambda b, c: (0, 0, 0)
    return pl.pallas_call(
        functools.partial(_ret_kernel, n_chunks=n_chunks),
        grid=(B, n_chunks),
        in_specs=[pl.BlockSpec((C, 1024), lambda b, c: (rows(b, c), 0)),
                  pl.BlockSpec((C, 1024), lambda b, c: (rows(b, c), 1)),
                  pl.BlockSpec((C, 2048), lambda b, c: (rows(b, c), 1)),
                  pl.BlockSpec((C, 2048), lambda b, c: (rows(b, c), 0)),
                  pl.BlockSpec((C, 2048), lambda b, c: (rows(b, c), 1)),
                  pl.BlockSpec((R_HEADS, C, C), const3),
                  pl.BlockSpec((R_HEADS, C, R_DV), const3),
                  pl.BlockSpec((R_HEADS, C, R_DK), const3),
                  pl.BlockSpec((R_HEADS, 1, R_DV), const3),
                  pl.BlockSpec((1, D_MODEL), lambda b, c: (0, 0)),
                  pl.BlockSpec((1, D_MODEL), lambda b, c: (0, 0)),
                  pl.BlockSpec((1, R_HEADS, R_DK, R_DV), lambda b, c: (b, 0, 0, 0))],
        out_specs=[pl.BlockSpec((C, D_MODEL), lambda b, c: (b * n_chunks + c, 0)),
                   pl.BlockSpec((1, R_HEADS, R_DK, R_DV), lambda b, c: (b, 0, 0, 0))],
        out_shape=[jax.ShapeDtypeStruct((B * T, D_MODEL), f32),
                   jax.ShapeDtypeStruct((B, R_HEADS, R_DK, R_DV), f32)],
        scratch_shapes=[pltpu.VMEM((R_HEADS, R_DK, R_DV), f32)],
        compiler_params=_cparams(("parallel", "arbitrary")),
        name="retention",
    )(g1, g1, g1, g2, g2, dm, cd, kd, gc, gn_g, b_r, s0)


def _bf16_split3(x):
    parts = []
    rem = np.float64(x)
    for _ in range(3):
        p = np.float64(np.asarray(rem, np.float32).astype(jnp.bfloat16).astype(np.float32))
        parts.append(p)
        rem = rem - p
    return parts


def _alibi_tables(TK):
    r = 1
    while TK // r > 256:
        r *= 2
    jj = np.arange(TK)
    kx = np.zeros((TK, LANES), np.float32)
    kx[:, 0:3] = (jj // r)[:, None]
    kx[:, 3:6] = (jj % r)[:, None]
    sig = np.zeros((A_HEADS, LANES), np.float32)
    sig_eff = []
    for h in range(A_HEADS):
        s3 = _bf16_split3(2.0 ** (-8.0 * (h + 1) / A_HEADS) * LOG2E)
        sig[h, 0:3] = [r * p for p in s3]
        sig[h, 3:6] = s3
        sig_eff.append(float(sum(s3)))
    vx = np.zeros((TK, LANES), np.float32)
    vx[:, 0] = 1.0
    return jnp.asarray(kx, bf16), jnp.asarray(vx, bf16), jnp.asarray(sig, f32), tuple(sig_eff)


def _key_to_float(u):
    key = u ^ jnp.int32(INT_MIN)
    bits = key ^ ((key >> 31) & jnp.int32(0x7FFFFFFF))
    return lax.bitcast_convert_type(bits, f32)


def _upper_half(x):
    bits = lax.bitcast_convert_type(x, jnp.int32) & jnp.int32(-65536)
    return lax.bitcast_convert_type(bits, f32)


def _knorm_kernel(k_ref, o_ref):
    rows = []
    for g in range(A_KV_HEADS):
        kg = k_ref[:, g * A_DH:(g + 1) * A_DH].astype(f32)
        n2 = jnp.max(jnp.sum(kg * kg, axis=1, keepdims=True), axis=0, keepdims=True)
        rows.append(jnp.broadcast_to(jnp.sqrt(n2), (1, LANES)))
    rows.append(jnp.zeros((8 - A_KV_HEADS, LANES), f32))
    o_ref[...] = jnp.concatenate(rows, axis=0)


def _knorm(k_all, TK):
    B, Lp, _ = k_all.shape
    nkb = Lp // TK
    return pl.pallas_call(
        _knorm_kernel,
        grid=(B, nkb),
        in_specs=[pl.BlockSpec((None, TK, 512), lambda b, kb: (b, kb, 0))],
        out_specs=pl.BlockSpec((None, 8, LANES), lambda b, kb: (b, kb, 0)),
        out_shape=jax.ShapeDtypeStruct((B, nkb * 8, LANES), f32),
        compiler_params=_cparams(("parallel", "parallel")),
        name="knorm",
    )(k_all)


def _dsa_kernel(si_ref, skb_ref, aq_ref, iq_ref, ikw_ref, kia_ref, k_ref, v_ref, kx_ref, vx_ref, sig_ref,
                km_ref, yr_ref, ga_ref, ba_ref, o_ref,
                qi_scr, qg_scr, isc_scr, hi_scr, thr_scr, jcut_scr, qn_scr, m_scr, acc_scr,
                *, TQ, TK, TS, L, P, n_sel, sig_eff, idx_bits):
    t = pl.program_id(1)
    i = si_ref[t]
    kb = skb_ref[t]
    r = TK // TS
    q0 = P + i * TQ
    lim_max = jnp.minimum(((q0 + TQ - 1) // CHUNK + 1) * CHUNK, L)
    nkb = (lim_max + TK - 1) // TK
    nsb = (lim_max + TS - 1) // TS
    pos = q0 + lax.broadcasted_iota(jnp.int32, (TQ, 1), 0)
    lim = jnp.minimum((lax.shift_right_logical(pos, 6) + 1) * CHUNK, L)

    def count_rows(pred):
        lane = lax.broadcasted_iota(jnp.int32, (TQ, LANES), 1)

        def block(sb, cnt):
            x = isc_scr[sb]
            for c in range(TS // LANES):
                hit = pred(x[:, c * LANES:(c + 1) * LANES], sb * TS + c * LANES + lane)
                cnt = cnt + jnp.where(hit, 1.0, 0.0)
            return cnt

        cnt = lax.fori_loop(0, nsb, block, jnp.zeros((TQ, LANES), f32))
        return jnp.sum(cnt, axis=1, keepdims=True)

    @pl.when(kb == nkb - 1)
    def _prepare():
        for h in range(I_HEADS):
            qi_scr[h * TQ:(h + 1) * TQ, :] = iq_ref[:, h * I_DH:(h + 1) * I_DH]
        for g in range(A_KV_HEADS):
            for hg in range(A_GROUP):
                h = g * A_GROUP + hg
                qh = aq_ref[:, h * A_DH:(h + 1) * A_DH]
                qg_scr[g, hg * TQ:(hg + 1) * TQ, 0:A_DH] = qh
                qg_scr[g, hg * TQ:(hg + 1) * TQ, A_DH:2 * A_DH] = jnp.broadcast_to(
                    sig_ref[h:h + 1, :], (TQ, LANES)).astype(bf16)
                qf = qh.astype(f32)
                qn_scr[h] = jnp.broadcast_to(jnp.sqrt(jnp.sum(qf * qf, axis=1, keepdims=True)), (TQ, LANES))
        w = ikw_ref[:, I_DH:I_DH + I_HEADS] * (I_DH ** -0.5 * I_HEADS ** -0.5)

        def score_block(sb, carry):
            start = pl.multiple_of(sb * TS, TS)
            ki = kia_ref[pl.ds(start, TS), :]
            logits = lax.dot_general(qi_scr[...], ki, (((1,), (1,)), ((), ())),
                                     preferred_element_type=f32)
            isc = jnp.zeros((TQ, TS), f32)
            for h in range(I_HEADS):
                isc = isc + jnp.maximum(logits[h * TQ:(h + 1) * TQ], 0.0) * w[:, h:h + 1]
            j = start + lax.broadcasted_iota(jnp.int32, (TQ, TS), 1)
            isc = jnp.where(j < lim, isc, -jnp.inf)
            isc_scr[sb] = isc
            hi_scr[sb] = _upper_half(isc).astype(bf16)
            return carry

        lax.fori_loop(0, nsb, score_block, 0)

        def fill_block(sb, carry):
            isc_scr[sb] = jnp.full((TQ, TS), -jnp.inf, f32)
            return carry

        lax.fori_loop(nsb, nkb * r, fill_block, 0)

        def hi_pass(b, carry):
            u, cu = carry
            cand = u | lax.shift_left(jnp.int32(1), 31 - b)
            tb = jnp.broadcast_to(_upper_half(_key_to_float(cand)), (TQ, LANES)).astype(bf16)

            def block(sb, cnt):
                x = hi_scr[sb]
                for c in range(TS // LANES):
                    cnt = cnt + jnp.where(x[:, c * LANES:(c + 1) * LANES] >= tb, one_b, zero_b)
                return cnt

            one_b, zero_b = jnp.ones((), bf16), jnp.zeros((), bf16)
            cnt = lax.fori_loop(0, nsb, block, jnp.zeros((TQ, LANES), bf16))
            total = jnp.sum(cnt.astype(f32), axis=1, keepdims=True)
            ok = total >= float(n_sel)
            return jnp.where(ok, cand, u), jnp.where(ok, total, cu)

        def bit_pass(b, carry):
            u, cu = carry
            cand = u | lax.shift_left(jnp.int32(1), 31 - b)
            tf = _key_to_float(cand)
            total = count_rows(lambda x, j: x >= tf)
            ok = total >= float(n_sel)
            return jnp.where(ok, cand, u), jnp.where(ok, total, cu)

        carry = lax.fori_loop(0, 16, hi_pass, (jnp.zeros((TQ, 1), jnp.int32), jnp.zeros((TQ, 1), f32)))
        u, cu = lax.fori_loop(16, 32, bit_pass, carry)
        thr = jnp.where(lim <= n_sel, F32_LOWEST, _key_to_float(u))
        thr_scr[...] = jnp.broadcast_to(thr, (TQ, LANES))
        jcut_scr[...] = jnp.full((TQ, LANES), JCUT_ALL, jnp.int32)
        m_scr[...] = jnp.full(m_scr.shape, NEG_BIG, f32)
        acc_scr[...] = jnp.zeros(acc_scr.shape, f32)

        tied = jnp.logical_and(lim > n_sel, cu > float(n_sel))

        @pl.when(jnp.max(jnp.where(tied, 1.0, 0.0)) > 0.0)
        def _ties():
            room = float(n_sel) - count_rows(lambda x, j: x > thr)

            def idx_pass(b, jc):
                cand = jc | lax.shift_left(jnp.int32(1), idx_bits - 1 - b)
                c = count_rows(lambda x, j: jnp.logical_and(x == thr, j < cand))
                return jnp.where(c <= room, cand, jc)

            jc = lax.fori_loop(0, idx_bits, idx_pass, jnp.zeros((TQ, 1), jnp.int32))
            jcut_scr[...] = jnp.broadcast_to(jnp.where(tied, jc, JCUT_ALL), (TQ, LANES))

    def attend(diag, g_first):
        x = jnp.concatenate([isc_scr[kb * r + c] for c in range(r)], axis=1)
        thr = thr_scr[:, 0:1]
        j = kb * TK + lax.broadcasted_iota(jnp.int32, (TQ, TK), 1)
        sel = jnp.logical_or(x > thr, jnp.logical_and(x == thr, j < jcut_scr[:, 0:1]))
        if diag:
            ahead = jnp.maximum(j - pos, 0).astype(f32)
        back = jnp.full((TQ, 1), kb * TK - q0, jnp.int32).astype(f32)
        PAIR = A_GROUP // 2
        for g in range(g_first, A_KV_HEADS):
            kg = jnp.concatenate([k_ref[:, g * A_DH:(g + 1) * A_DH], kx_ref[...]], axis=1)
            vg = jnp.concatenate([v_ref[:, g * A_DH:(g + 1) * A_DH], vx_ref[...]], axis=1)
            for half in range(A_GROUP // PAIR):
                rows = slice(half * PAIR * TQ, (half + 1) * PAIR * TQ)
                s = lax.dot_general(qg_scr[g, rows, :], kg, (((1,), (1,)), ((), ())),
                                    preferred_element_type=f32)
                ps, alphas = [], []
                for hp in range(PAIR):
                    h = g * A_GROUP + half * PAIR + hp
                    sh = s[hp * TQ:(hp + 1) * TQ]
                    if diag:
                        sh = sh - (2.0 * sig_eff[h]) * ahead
                    sh = jnp.where(sel, sh, NEG_BIG)
                    d = back * sig_eff[h]
                    m_prev = m_scr[h]
                    m_new = jnp.maximum(m_prev, jnp.max(sh, axis=1, keepdims=True) + d)
                    ps.append(jnp.exp2(sh - (m_new[:, 0:1] - d)))
                    alphas.append(jnp.exp2(m_prev - m_new))
                    m_scr[h] = m_new
                p = jnp.concatenate(ps, axis=0).astype(bf16)
                a = jnp.concatenate(alphas, axis=0)
                acc_scr[g, rows, :] = (jnp.concatenate([a, a], axis=1) * acc_scr[g, rows, :]
                                       + jnp.dot(p, vg, preferred_element_type=f32))

    is_diag = kb * TK + TK - 1 > q0

    @pl.when(is_diag)
    def _():
        attend(True, 0)

    @pl.when(jnp.logical_not(is_diag))
    def _():
        back1 = jnp.full((1, LANES), kb * TK - q0, jnp.int32).astype(f32)

        def needed(g):
            worst = None
            for hg in range(A_GROUP):
                h = g * A_GROUP + hg
                over = (qn_scr[h] * (km_ref[g:g + 1, :] * 1.001)
                        + (1.0 + sig_eff[h] * (TK - 1) + UNDERFLOW_LOG2) + back1 * sig_eff[h] - m_scr[h])
                worst = over if worst is None else jnp.maximum(worst, over)
            return jnp.max(worst) > 0.0

        g_first = jnp.int32(A_KV_HEADS - 1)
        for g in range(A_KV_HEADS - 2, -1, -1):
            g_first = jnp.where(needed(g), g, g_first)
        for gs in range(A_KV_HEADS):
            @pl.when(g_first == gs)
            def _():
                attend(False, gs)

    @pl.when(kb == 0)
    def _finish():
        gate = _sigmoid(ga_ref[...] + ba_ref[...])
        for g in range(A_KV_HEADS):
            acc = acc_scr[g]
            og = acc[:, 0:A_DH] / acc[:, A_DH:A_DH + 1]
            for hg in range(A_GROUP):
                cols = slice((g * A_GROUP + hg) * A_DH, (g * A_GROUP + hg + 1) * A_DH)
                ya = og[hg * TQ:(hg + 1) * TQ]
                o_ref[:, cols] = (yr_ref[:, cols] + gate[:, cols] * ya).astype(o_ref.dtype)


def _dsa(g1, g2, ikw, kia, k_all, v_all, yr, b_a, row0, B, T, P, L, TQ, TK, TS):
    Lp = kia.shape[1]
    nkb_total = Lp // TK
    nq = T // TQ
    rb0 = row0 // TQ
    n_sel = min(TOPK_MAX, L // 4)
    steps = []
    for i in range(nq):
        lim_max = min(((P + i * TQ + TQ - 1) // CHUNK + 1) * CHUNK, L)
        steps += [(i, kb) for kb in reversed(range(-(-lim_max // TK)))]
    step_i = jnp.asarray([s[0] for s in steps], jnp.int32)
    step_kb = jnp.asarray([s[1] for s in steps], jnp.int32)
    kx, vx, sig, sig_eff = _alibi_tables(TK)
    kmax = _knorm(k_all, TK)
    rows = lambda b, t, si: rb0 + b * nq + si[t]
    kern = functools.partial(_dsa_kernel, TQ=TQ, TK=TK, TS=TS, L=L, P=P, n_sel=n_sel, sig_eff=sig_eff,
                             idx_bits=int(Lp).bit_length())
    grid_spec = pltpu.PrefetchScalarGridSpec(
        num_scalar_prefetch=2,
        grid=(B, len(steps)),
        in_specs=[pl.BlockSpec((TQ, 2048), lambda b, t, si, sk: (rows(b, t, si), 2)),
                  pl.BlockSpec((TQ, 1024), lambda b, t, si, sk: (rows(b, t, si), 6)),
                  pl.BlockSpec((TQ, LANES), lambda b, t, si, sk: (rows(b, t, si), 0)),
                  pl.BlockSpec((None, Lp, I_DH), lambda b, t, si, sk: (b, 0, 0)),
                  pl.BlockSpec((None, TK, 512), lambda b, t, si, sk: (b, sk[t], 0)),
                  pl.BlockSpec((None, TK, 512), lambda b, t, si, sk: (b, sk[t], 0)),
                  pl.BlockSpec((TK, LANES), lambda b, t, si, sk: (0, 0)),
                  pl.BlockSpec((TK, LANES), lambda b, t, si, sk: (0, 0)),
                  pl.BlockSpec((A_HEADS, LANES), lambda b, t, si, sk: (0, 0)),
                  pl.BlockSpec((None, 8, LANES), lambda b, t, si, sk: (b, sk[t], 0)),
                  pl.BlockSpec((TQ, D_MODEL), lambda b, t, si, sk: (b * nq + si[t], 0)),
                  pl.BlockSpec((TQ, 2048), lambda b, t, si, sk: (rows(b, t, si), 2)),
                  pl.BlockSpec((1, D_MODEL), lambda b, t, si, sk: (0, 0))],
        out_specs=pl.BlockSpec((TQ, D_MODEL), lambda b, t, si, sk: (b * nq + si[t], 0)),
        scratch_shapes=[pltpu.VMEM((I_HEADS * TQ, I_DH), bf16),
                        pltpu.VMEM((A_KV_HEADS, A_GROUP * TQ, 2 * A_DH), bf16),
                        pltpu.VMEM((nkb_total * (TK // TS), TQ, TS), f32),
                        pltpu.VMEM((nkb_total * (TK // TS), TQ, TS), bf16),
                        pltpu.VMEM((TQ, LANES), f32),
                        pltpu.VMEM((TQ, LANES), jnp.int32),
                        pltpu.VMEM((A_HEADS, TQ, LANES), f32),
                        pltpu.VMEM((A_HEADS, TQ, LANES), f32),
                        pltpu.VMEM((A_KV_HEADS, A_GROUP * TQ, 2 * A_DH), f32)])
    return pl.pallas_call(
        kern,
        grid_spec=grid_spec,
        out_shape=jax.ShapeDtypeStruct((B * T, D_MODEL), bf16),
        compiler_params=_cparams(("parallel", "arbitrary")),
        name="dsa",
    )(step_i, step_kb, g1, g1, ikw, kia, k_all, v_all, kx, vx, sig, kmax, yr, g2, b_a)


def _mm_ln_kernel(x_ref, w_ref, r_ref, g_ref, b_ref, o_ref, ob_ref, acc_scr, *, nk):
    kk = pl.program_id(1)

    @pl.when(kk == 0)
    def _():
        acc_scr[...] = jnp.zeros(acc_scr.shape, f32)

    acc_scr[...] += jnp.dot(x_ref[...], w_ref[...], preferred_element_type=f32)

    @pl.when(kk == nk - 1)
    def _():
        y = ALPHA * r_ref[...] + acc_scr[...]
        mu = jnp.mean(y, axis=-1, keepdims=True)
        d = y - mu
        var = jnp.mean(d * d, axis=-1, keepdims=True)
        out = d * lax.rsqrt(var + LN_EPS) * g_ref[...] + b_ref[...]
        o_ref[...] = out
        ob_ref[...] = out.astype(bf16)


def _matmul_ln(x, w, resid, gamma, beta):
    M, K = x.shape
    N = w.shape[1]
    tm = _pick(M, (512, 384, 256, 128, 64))
    tk = _pick(K, (2048, 1408, 512, 256, 128))
    nk = K // tk
    return pl.pallas_call(
        functools.partial(_mm_ln_kernel, nk=nk),
        grid=(M // tm, nk),
        in_specs=[pl.BlockSpec((tm, tk), lambda m, k: (m, k)),
                  pl.BlockSpec((tk, N), lambda m, k: (k, 0)),
                  pl.BlockSpec((tm, N), lambda m, k: (m, 0)),
                  pl.BlockSpec((1, N), lambda m, k: (0, 0)),
                  pl.BlockSpec((1, N), lambda m, k: (0, 0))],
        out_specs=[pl.BlockSpec((tm, N), lambda m, k: (m, 0)),
                   pl.BlockSpec((tm, N), lambda m, k: (m, 0))],
        out_shape=[jax.ShapeDtypeStruct((M, N), f32), jax.ShapeDtypeStruct((M, N), bf16)],
        scratch_shapes=[pltpu.VMEM((tm, N), f32)],
        compiler_params=_cparams(("parallel", "arbitrary")),
        name="proj_ln",
    )(x, w, resid, gamma, beta)


def _gelu_tanh(x):
    return 0.5 * x * (1.0 + jnp.tanh(0.7978845608028654 * (x + 0.044715 * (x * x * x))))


def _up_kernel(x_ref, wh_ref, wu_ref, cw_ref, cb_ref, buf_ref, o_ref, nb_ref, carry_scr, *, tm, n_mt):
    m = pl.program_id(2)

    @pl.when(m == 0)
    def _():
        carry_scr[0:8 - (CONV_W - 1), :] = jnp.zeros((8 - (CONV_W - 1), carry_scr.shape[1]), f32)
        carry_scr[8 - (CONV_W - 1):8, :] = buf_ref[0]

    x = x_ref[...]
    h = jnp.dot(x, wh_ref[...], preferred_element_type=f32)
    u = jnp.dot(x, wu_ref[...], preferred_element_type=f32)
    prev = carry_scr[...]
    row = lax.broadcasted_iota(jnp.int32, h.shape, 0)
    hc = cb_ref[...] + cw_ref[CONV_W - 1:CONV_W, :] * h
    for s in range(1, CONV_W):
        shifted = pltpu.roll(h, s, 0)
        head = pltpu.roll(prev, s, 0)
        head = jnp.concatenate([head] * (tm // 8), axis=0)
        shifted = jnp.where(row < s, head, shifted)
        hc = hc + cw_ref[CONV_W - 1 - s:CONV_W - s, :] * shifted
    o_ref[...] = (_gelu_tanh(hc) * u).astype(o_ref.dtype)
    carry_scr[...] = h[tm - 8:tm, :]

    @pl.when(m == n_mt - 1)
    def _():
        nb_ref[0] = h[tm - (CONV_W - 1):tm, :]


def _up_conv(x, w_h, w_u, conv_w, conv_b, buf, row0, B, T):
    K = x.shape[1]
    tm = _pick(T, (512, 256, 128, 64))
    tn = 512
    n_mt = T // tm
    rb0 = row0 // tm
    return pl.pallas_call(
        functools.partial(_up_kernel, tm=tm, n_mt=n_mt),
        grid=(D_FF // tn, B, n_mt),
        in_specs=[pl.BlockSpec((tm, K), lambda n, b, m: (rb0 + b * n_mt + m, 0)),
                  pl.BlockSpec((K, tn), lambda n, b, m: (0, n)),
                  pl.BlockSpec((K, tn), lambda n, b, m: (0, n)),
                  pl.BlockSpec((CONV_W, tn), lambda n, b, m: (0, n)),
                  pl.BlockSpec((1, tn), lambda n, b, m: (0, n)),
                  pl.BlockSpec((1, CONV_W - 1, tn), lambda n, b, m: (b, 0, n))],
        out_specs=[pl.BlockSpec((tm, tn), lambda n, b, m: (b * n_mt + m, n)),
                   pl.BlockSpec((1, CONV_W - 1, tn), lambda n, b, m: (b, 0, n))],
        out_shape=[jax.ShapeDtypeStruct((B * T, D_FF), bf16),
                   jax.ShapeDtypeStruct((B, CONV_W - 1, D_FF), f32)],
        scratch_shapes=[pltpu.VMEM((8, tn), f32)],
        compiler_params=_cparams(("parallel", "arbitrary", "arbitrary")),
        name="up_conv",
    )(x, w_h, w_u, conv_w, conv_b, buf)


def _pad_keys(a, lp):
    return jnp.pad(a, ((0, 0), (0, lp - a.shape[1]), (0, 0)))


def kernel(x_prompt, x_sample, cache_attn_k, cache_attn_v, cache_idx_k, state_ret, state_conv,
           w_in, b_gate, gn_g, w_out, ln1_g, ln1_b, w_up, conv_w, conv_b, w_down, ln2_g, ln2_b):
    Bp, Tp, _ = x_prompt.shape
    Bs, Ts, _ = x_sample.shape
    P = cache_attn_k.shape[2]
    Mp, Ms = Bp * Tp, Bs * Ts
    M = Mp + Ms
    x_all = jnp.concatenate([x_prompt.reshape(Mp, D_MODEL), x_sample.reshape(Ms, D_MODEL)], axis=0)
    xb = x_all.astype(bf16)

    splits = (R_HEADS * R_DK, R_HEADS * R_DK, D_MODEL, D_MODEL, A_HEADS * A_DH, A_KV_HEADS * A_DH,
              A_KV_HEADS * A_DH, I_HEADS * I_DH, I_DH, I_HEADS, D_MODEL, D_MODEL)
    off = np.concatenate([[0], np.cumsum(splits)])
    wi = w_in[0]
    col = lambda i: wi[:, off[i]:off[i + 1]]
    rq, rk, rv, rg, aq, ak, av, iq, ik, iw, g_r, g_a = range(12)
    w1 = jnp.concatenate([col(rq), col(rk), col(rv), col(aq), col(iq)], axis=1).astype(bf16)
    s1 = jnp.concatenate([jnp.ones((1024,), f32), jnp.full((1024,), R_DK ** -0.5, f32),
                          jnp.ones((2048,), f32), jnp.full((2048,), A_DH ** -0.5 * LOG2E, f32),
                          jnp.ones((1024,), f32)])[None]
    w2 = jnp.concatenate([col(rg), col(g_r), col(g_a)], axis=1).astype(bf16)
    w3 = jnp.concatenate([col(ak), col(av)], axis=1).astype(bf16)
    w4 = jnp.pad(jnp.concatenate([col(ik), col(iw)], axis=1),
                 ((0, 0), (0, LANES - I_DH - I_HEADS))).astype(bf16)
    g1 = _matmul(xb, w1, s1, bf16, 1024)
    g2 = _matmul(xb, w2, jnp.ones((1, w2.shape[1]), f32), f32, 1024)
    kv = _matmul(xb, w3, jnp.ones((1, w3.shape[1]), f32), f32, 512)
    ikw = _matmul(xb, w4, jnp.ones((1, LANES), f32), f32, LANES)

    b_r = b_gate[0, :D_MODEL][None]
    b_a = b_gate[0, D_MODEL:][None]
    gn = gn_g[0][None]

    yr_p, sn_p = _retention(g1, g2, gn, b_r, jnp.zeros((Bp, R_HEADS, R_DK, R_DV), f32), 0, Bp, Tp)
    yr_s, sn_s = _retention(g1, g2, gn, b_r, state_ret[0].astype(f32), Mp, Bs, Ts)

    kvb = kv.astype(bf16)
    ikb = ikw[:, :I_DH].astype(bf16)
    tq_p = _pick(Tp, (128,))
    tk_p = _pick(Tp, (1024, 512, 256, 128))
    ts_p = min(tk_p, 512)
    mix_p = _dsa(g1, g2, ikw, ikb[:Mp].reshape(Bp, Tp, I_DH), kvb[:Mp, :512].reshape(Bp, Tp, 512),
                 kvb[:Mp, 512:].reshape(Bp, Tp, 512), yr_p, b_a, 0, Bp, Tp, 0, Tp, tq_p, tk_p, ts_p)
    Ls = P + Ts
    lp = -(-Ls // LANES) * LANES
    ki_s = jnp.concatenate([cache_idx_k[0].astype(bf16), ikb[Mp:].reshape(Bs, Ts, I_DH)], axis=1)
    k_s = jnp.concatenate([cache_attn_k[0].reshape(Bs, P, 512).astype(bf16),
                           kvb[Mp:, :512].reshape(Bs, Ts, 512)], axis=1)
    v_s = jnp.concatenate([cache_attn_v[0].reshape(Bs, P, 512).astype(bf16),
                           kvb[Mp:, 512:].reshape(Bs, Ts, 512)], axis=1)
    mix_s = _dsa(g1, g2, ikw, _pad_keys(ki_s, lp), _pad_keys(k_s, lp), _pad_keys(v_s, lp), yr_s, b_a,
                 Mp, Bs, Ts, P, Ls, Ts, lp, lp)

    wo, wd = w_out[0].astype(bf16), w_down[0].astype(bf16)
    wu = w_up[0].astype(bf16)
    w_h, w_u = wu[:, :D_FF], wu[:, D_FF:]
    g_1, b_1, g_2, b_2 = ln1_g[0][None], ln1_b[0][None], ln2_g[0][None], ln2_b[0][None]
    x1_p, x1b_p = _matmul_ln(mix_p, wo, x_prompt.reshape(Mp, D_MODEL), g_1, b_1)
    x1_s, x1b_s = _matmul_ln(mix_s, wo, x_sample.reshape(Ms, D_MODEL), g_1, b_1)
    ff_p, nb_p = _up_conv(x1b_p, w_h, w_u, conv_w[0], conv_b[0][None],
                          jnp.zeros((Bp, CONV_W - 1, D_FF), f32), 0, Bp, Tp)
    ff_s, nb_s = _up_conv(x1b_s, w_h, w_u, conv_w[0], conv_b[0][None], state_conv[0].astype(f32), 0, Bs, Ts)
    y_p, _ = _matmul_ln(ff_p, wd, x1_p, g_2, b_2)
    y_s, _ = _matmul_ln(ff_s, wd, x1_s, g_2, b_2)

    dt = x_prompt.dtype
    return (y_p.reshape(Bp, Tp, D_MODEL).astype(dt),
            y_s.reshape(Bs, Ts, D_MODEL).astype(dt),
            kv[:Mp, :512].reshape(1, Bp, Tp, A_KV_HEADS, A_DH).astype(dt),
            kv[:Mp, 512:].reshape(1, Bp, Tp, A_KV_HEADS, A_DH).astype(dt),
            ikw[:Mp, :I_DH].reshape(1, Bp, Tp, I_DH).astype(dt),
            sn_p[None].astype(dt),
            nb_p[None].astype(dt),
            kv[Mp:, :512].reshape(1, Bs, Ts, A_KV_HEADS, A_DH).astype(dt),
            kv[Mp:, 512:].reshape(1, Bs, Ts, A_KV_HEADS, A_DH).astype(dt),
            ikw[Mp:, :I_DH].reshape(1, Bs, Ts, I_DH).astype(dt),
            sn_s[None].astype(dt),
            nb_s[None].astype(dt))
```

```python
import functools

import numpy as np
import jax
import jax.numpy as jnp
from jax import lax
from jax.experimental import pallas as pl
from jax.experimental.pallas import tpu as pltpu

f32 = jnp.float32
bf16 = jnp.bfloat16

D_MODEL = 2048
CHUNK = 64
R_HEADS = 8
R_DK = 128
R_DV = 256
A_HEADS = 16
A_KV_HEADS = 4
A_DH = 128
A_GROUP = 4
I_HEADS = 16
I_DH = 64
TOPK_MAX = 256
D_FF = 5632
CONV_W = 3
DEPTH = 1
ALPHA = (2 * DEPTH) ** 0.25
LN_EPS = 1e-5

LANES = 128
VMEM_LIMIT = 56 * 1024 * 1024
NEG_BIG = -1e30
F32_LOWEST = -3.4028234663852886e38
INT_MIN = -2 ** 31
LOG2E = 1.4426950408889634
JCUT_ALL = 2 ** 30
UNDERFLOW_LOG2 = 160.0


def _pick(n, cands):
    for c in cands:
        if n % c == 0:
            return c
    raise ValueError(f"no tile in {cands} divides {n}")


def _cparams(sem):
    return pltpu.CompilerParams(dimension_semantics=sem, vmem_limit_bytes=VMEM_LIMIT)


def _sigmoid(x):
    return 1.0 / (1.0 + jnp.exp(-x))


def _mm_kernel(x_ref, w_ref, s_ref, o_ref):
    acc = jnp.dot(x_ref[...], w_ref[...], preferred_element_type=f32)
    o_ref[...] = (acc * s_ref[...]).astype(o_ref.dtype)


def _matmul(x, w, scale, out_dtype, tn):
    M, K = x.shape
    N = w.shape[1]
    tm = _pick(M, (1536, 1024, 512, 384, 256, 128, 64))
    return pl.pallas_call(
        _mm_kernel,
        grid=(N // tn, M // tm),
        in_specs=[pl.BlockSpec((tm, K), lambda n, m: (m, 0)),
                  pl.BlockSpec((K, tn), lambda n, m: (0, n)),
                  pl.BlockSpec((1, tn), lambda n, m: (0, n))],
        out_specs=pl.BlockSpec((tm, tn), lambda n, m: (m, n)),
        out_shape=jax.ShapeDtypeStruct((M, N), out_dtype),
        compiler_params=_cparams(("parallel", "parallel")),
        name="in_proj",
    )(x, w, scale)


def _ret_kernel(q_ref, k_ref, v_ref, rg_ref, gr_ref, dm_ref, cd_ref, kd_ref, gc_ref, gn_ref, br_ref,
                s0_ref, y_ref, sn_ref, s_scr, *, n_chunks):
    c = pl.program_id(1)

    @pl.when(c == 0)
    def _():
        s_scr[...] = s0_ref[0]

    for h in range(R_HEADS):
        q = q_ref[:, h * R_DK:(h + 1) * R_DK]
        k = k_ref[:, h * R_DK:(h + 1) * R_DK]
        v = v_ref[:, h * R_DV:(h + 1) * R_DV]
        s_old = s_scr[h]
        sc = lax.dot_general(q, k, (((1,), (1,)), ((), ())), preferred_element_type=f32) * dm_ref[h]
        inner = jnp.dot(sc.astype(bf16), v, preferred_element_type=f32)
        cross = jnp.dot(q, s_old.astype(bf16), preferred_element_type=f32) * cd_ref[h]
        o = inner + cross
        kdk = (k.astype(f32) * kd_ref[h]).astype(bf16)
        upd = lax.dot_general(kdk, v, (((0,), (0,)), ((), ())), preferred_element_type=f32)
        s_scr[h] = gc_ref[h] * s_old + upd
        mu = jnp.mean(o, axis=-1, keepdims=True)
        d = o - mu
        var = jnp.mean(d * d, axis=-1, keepdims=True)
        cols = slice(h * R_DV, (h + 1) * R_DV)
        on = d * lax.rsqrt(var + LN_EPS) * gn_ref[:, cols]
        rg = rg_ref[:, cols]
        gate = _sigmoid(gr_ref[:, cols] + br_ref[:, cols])
        y_ref[:, cols] = gate * (rg * _sigmoid(rg) * on)

    @pl.when(c == n_chunks - 1)
    def _():
        sn_ref[0] = s_scr[...]


def _retention_tables(C):
    lg = np.log1p(-np.exp2(-5.0 - np.arange(R_HEADS, dtype=np.float64)))
    i = np.arange(C, dtype=np.float64)
    diff = i[:, None] - i[None, :]
    dm = np.where(diff >= 0, np.exp(lg[:, None, None] * np.maximum(diff, 0.0)), 0.0)
    cd = np.exp((i + 1.0)[None, :] * lg[:, None])
    kd = np.exp((C - 1.0 - i)[None, :] * lg[:, None])
    gc = np.exp(C * lg)
    return (jnp.asarray(dm, f32),
            jnp.asarray(np.broadcast_to(cd[:, :, None], (R_HEADS, C, R_DV)), f32),
            jnp.asarray(np.broadcast_to(kd[:, :, None], (R_HEADS, C, R_DK)), f32),
            jnp.asarray(np.broadcast_to(gc[:, None, None], (R_HEADS, 1, R_DV)), f32))


def _retention(g1, g2, gn_g, b_r, s0, row0, B, T):
    C = min(T, 256)
    n_chunks = T // C
    rb0 = row0 // C
    dm, cd, kd, gc = _retention_tables(C)
    rows = lambda b, c: rb0 + b * n_chunks + c
    const3 = lambda b, c: (0, 0, 0)
    return pl.pallas_call(
        functools.partial(_ret_kernel, n_chunks=n_chunks),
        grid=(B, n_chunks),
        in_specs=[pl.BlockSpec((C, 1024), lambda b, c: (rows(b, c), 0)),
                  pl.BlockSpec((C, 1024), lambda b, c: (rows(b, c), 1)),
                  pl.BlockSpec((C, 2048), lambda b, c: (rows(b, c), 1)),
                  pl.BlockSpec((C, 2048), lambda b, c: (rows(b, c), 0)),
                  pl.BlockSpec((C, 2048), lambda b, c: (rows(b, c), 1)),
                  pl.BlockSpec((R_HEADS, C, C), const3),
                  pl.BlockSpec((R_HEADS, C, R_DV), const3),
                  pl.BlockSpec((R_HEADS, C, R_DK), const3),
                  pl.BlockSpec((R_HEADS, 1, R_DV), const3),
                  pl.BlockSpec((1, D_MODEL), lambda b, c: (0, 0)),
                  pl.BlockSpec((1, D_MODEL), lambda b, c: (0, 0)),
                  pl.BlockSpec((1, R_HEADS, R_DK, R_DV), lambda b, c: (b, 0, 0, 0))],
        out_specs=[pl.BlockSpec((C, D_MODEL), lambda b, c: (b * n_chunks + c, 0)),
                   pl.BlockSpec((1, R_HEADS, R_DK, R_DV), lambda b, c: (b, 0, 0, 0))],
        out_shape=[jax.ShapeDtypeStruct((B * T, D_MODEL), f32),
                   jax.ShapeDtypeStruct((B, R_HEADS, R_DK, R_DV), f32)],
        scratch_shapes=[pltpu.VMEM((R_HEADS, R_DK, R_DV), f32)],
        compiler_params=_cparams(("parallel", "arbitrary")),
        name="retention",
    )(g1, g1, g1, g2, g2, dm, cd, kd, gc, gn_g, b_r, s0)


def _bf16_split3(x):
    parts = []
    rem = np.float64(x)
    for _ in range(3):
        p = np.float64(np.asarray(rem, np.float32).astype(jnp.bfloat16).astype(np.float32))
        parts.append(p)
        rem = rem - p
    return parts


def _alibi_tables(TK):
    r = 1
    while TK // r > 256:
        r *= 2
    jj = np.arange(TK)
    kx = np.zeros((TK, LANES), np.float32)
    kx[:, 0:3] = (jj // r)[:, None]
    kx[:, 3:6] = (jj % r)[:, None]
    sig = np.zeros((A_HEADS, LANES), np.float32)
    sig_eff = []
    for h in range(A_HEADS):
        s3 = _bf16_split3(2.0 ** (-8.0 * (h + 1) / A_HEADS) * LOG2E)
        sig[h, 0:3] = [r * p for p in s3]
        sig[h, 3:6] = s3
        sig_eff.append(float(sum(s3)))
    vx = np.zeros((TK, LANES), np.float32)
    vx[:, 0] = 1.0
    return jnp.asarray(kx, bf16), jnp.asarray(vx, bf16), jnp.asarray(sig, f32), tuple(sig_eff)


def _key_to_float(u):
    key = u ^ jnp.int32(INT_MIN)
    bits = key ^ ((key >> 31) & jnp.int32(0x7FFFFFFF))
    return lax.bitcast_convert_type(bits, f32)


def _upper_half(x):
    bits = lax.bitcast_convert_type(x, jnp.int32) & jnp.int32(-65536)
    return lax.bitcast_convert_type(bits, f32)


def _knorm_kernel(k_ref, o_ref):
    rows = []
    for g in range(A_KV_HEADS):
        kg = k_ref[:, g * A_DH:(g + 1) * A_DH].astype(f32)
        n2 = jnp.max(jnp.sum(kg * kg, axis=1, keepdims=True), axis=0, keepdims=True)
        rows.append(jnp.broadcast_to(jnp.sqrt(n2), (1, LANES)))
    rows.append(jnp.zeros((8 - A_KV_HEADS, LANES), f32))
    o_ref[...] = jnp.concatenate(rows, axis=0)


def _knorm(k_all, TK):
    B, Lp, _ = k_all.shape
    nkb = Lp // TK
    return pl.pallas_call(
        _knorm_kernel,
        grid=(B, nkb),
        in_specs=[pl.BlockSpec((None, TK, 512), lambda b, kb: (b, kb, 0))],
        out_specs=pl.BlockSpec((None, 8, LANES), lambda b, kb: (b, kb, 0)),
        out_shape=jax.ShapeDtypeStruct((B, nkb * 8, LANES), f32),
        compiler_params=_cparams(("parallel", "parallel")),
        name="knorm",
    )(k_all)


def _dsa_kernel(si_ref, skb_ref, aq_ref, iq_ref, ikw_ref, kia_ref, k_ref, v_ref, kx_ref, vx_ref, sig_ref,
                km_ref, yr_ref, ga_ref, ba_ref, o_ref,
                qi_scr, qg_scr, isc_scr, hi_scr, thr_scr, jcut_scr, qn_scr, m_scr, acc_scr,
                *, TQ, TK, TS, L, P, n_sel, sig_eff, idx_bits):
    t = pl.program_id(1)
    i = si_ref[t]
    kb = skb_ref[t]
    r = TK // TS
    q0 = P + i * TQ
    lim_max = jnp.minimum(((q0 + TQ - 1) // CHUNK + 1) * CHUNK, L)
    nkb = (lim_max + TK - 1) // TK
    nsb = (lim_max + TS - 1) // TS
    pos = q0 + lax.broadcasted_iota(jnp.int32, (TQ, 1), 0)
    lim = jnp.minimum((lax.shift_right_logical(pos, 6) + 1) * CHUNK, L)

    def count_rows(pred):
        lane = lax.broadcasted_iota(jnp.int32, (TQ, LANES), 1)

        def block(sb, cnt):
            x = isc_scr[sb]
            for c in range(TS // LANES):
                hit = pred(x[:, c * LANES:(c + 1) * LANES], sb * TS + c * LANES + lane)
                cnt = cnt + jnp.where(hit, 1.0, 0.0)
            return cnt

        cnt = lax.fori_loop(0, nsb, block, jnp.zeros((TQ, LANES), f32))
        return jnp.sum(cnt, axis=1, keepdims=True)

    @pl.when(kb == nkb - 1)
    def _prepare():
        for h in range(I_HEADS):
            qi_scr[h * TQ:(h + 1) * TQ, :] = iq_ref[:, h * I_DH:(h + 1) * I_DH]
        for g in range(A_KV_HEADS):
            for hg in range(A_GROUP):
                h = g * A_GROUP + hg
                qh = aq_ref[:, h * A_DH:(h + 1) * A_DH]
                qg_scr[g, hg * TQ:(hg + 1) * TQ, 0:A_DH] = qh
                qg_scr[g, hg * TQ:(hg + 1) * TQ, A_DH:2 * A_DH] = jnp.broadcast_to(
                    sig_ref[h:h + 1, :], (TQ, LANES)).astype(bf16)
                qf = qh.astype(f32)
                qn_scr[h] = jnp.broadcast_to(jnp.sqrt(jnp.sum(qf * qf, axis=1, keepdims=True)), (TQ, LANES))
        w = ikw_ref[:, I_DH:I_DH + I_HEADS] * (I_DH ** -0.5 * I_HEADS ** -0.5)

        def score_block(sb, carry):
            start = pl.multiple_of(sb * TS, TS)
            ki = kia_ref[pl.ds(start, TS), :]
            logits = lax.dot_general(qi_scr[...], ki, (((1,), (1,)), ((), ())),
                                     preferred_element_type=f32)
            isc = jnp.zeros((TQ, TS), f32)
            for h in range(I_HEADS):
                isc = isc + jnp.maximum(logits[h * TQ:(h + 1) * TQ], 0.0) * w[:, h:h + 1]
            j = start + lax.broadcasted_iota(jnp.int32, (TQ, TS), 1)
            isc = jnp.where(j < lim, isc, -jnp.inf)
            isc_scr[sb] = isc
            hi_scr[sb] = _upper_half(isc).astype(bf16)
            return carry

        lax.fori_loop(0, nsb, score_block, 0)

        def fill_block(sb, carry):
            isc_scr[sb] = jnp.full((TQ, TS), -jnp.inf, f32)
            return carry

        lax.fori_loop(nsb, nkb * r, fill_block, 0)

        def hi_pass(b, carry):
            u, cu = carry
            cand = u | lax.shift_left(jnp.int32(1), 31 - b)
            tb = jnp.broadcast_to(_upper_half(_key_to_float(cand)), (TQ, LANES)).astype(bf16)

            def block(sb, cnt):
                x = hi_scr[sb]
                for c in range(TS // LANES):
                    cnt = cnt + jnp.where(x[:, c * LANES:(c + 1) * LANES] >= tb, one_b, zero_b)
                return cnt

            one_b, zero_b = jnp.ones((), bf16), jnp.zeros((), bf16)
            cnt = lax.fori_loop(0, nsb, block, jnp.zeros((TQ, LANES), bf16))
            total = jnp.sum(cnt.astype(f32), axis=1, keepdims=True)
            ok = total >= float(n_sel)
            return jnp.where(ok, cand, u), jnp.where(ok, total, cu)

        def bit_pass(b, carry):
            u, cu = carry
            cand = u | lax.shift_left(jnp.int32(1), 31 - b)
            tf = _key_to_float(cand)
            total = count_rows(lambda x, j: x >= tf)
            ok = total >= float(n_sel)
            return jnp.where(ok, cand, u), jnp.where(ok, total, cu)

        carry = lax.fori_loop(0, 16, hi_pass, (jnp.zeros((TQ, 1), jnp.int32), jnp.zeros((TQ, 1), f32)))
        u, cu = lax.fori_loop(16, 32, bit_pass, carry)
        thr = jnp.where(lim <= n_sel, F32_LOWEST, _key_to_float(u))
        thr_scr[...] = jnp.broadcast_to(thr, (TQ, LANES))
        jcut_scr[...] = jnp.full((TQ, LANES), JCUT_ALL, jnp.int32)
        m_scr[...] = jnp.full(m_scr.shape, NEG_BIG, f32)
        acc_scr[...] = jnp.zeros(acc_scr.shape, f32)

        tied = jnp.logical_and(lim > n_sel, cu > float(n_sel))

        @pl.when(jnp.max(jnp.where(tied, 1.0, 0.0)) > 0.0)
        def _ties():
            room = float(n_sel) - count_rows(lambda x, j: x > thr)

            def idx_pass(b, jc):
                cand = jc | lax.shift_left(jnp.int32(1), idx_bits - 1 - b)
                c = count_rows(lambda x, j: jnp.logical_and(x == thr, j < cand))
                return jnp.where(c <= room, cand, jc)

            jc = lax.fori_loop(0, idx_bits, idx_pass, jnp.zeros((TQ, 1), jnp.int32))
            jcut_scr[...] = jnp.broadcast_to(jnp.where(tied, jc, JCUT_ALL), (TQ, LANES))

    def attend(diag, g_first):
        x = jnp.concatenate([isc_scr[kb * r + c] for c in range(r)], axis=1)
        thr = thr_scr[:, 0:1]
        j = kb * TK + lax.broadcasted_iota(jnp.int32, (TQ, TK), 1)
        sel = jnp.logical_or(x > thr, jnp.logical_and(x == thr, j < jcut_scr[:, 0:1]))
        if diag:
            ahead = jnp.maximum(j - pos, 0).astype(f32)
        back = jnp.full((TQ, 1), kb * TK - q0, jnp.int32).astype(f32)
        PAIR = A_GROUP // 2
        for g in range(g_first, A_KV_HEADS):
            kg = jnp.concatenate([k_ref[:, g * A_DH:(g + 1) * A_DH], kx_ref[...]], axis=1)
            vg = jnp.concatenate([v_ref[:, g * A_DH:(g + 1) * A_DH], vx_ref[...]], axis=1)
            for half in range(A_GROUP // PAIR):
                rows = slice(half * PAIR * TQ, (half + 1) * PAIR * TQ)
                s = lax.dot_general(qg_scr[g, rows, :], kg, (((1,), (1,)), ((), ())),
                                    preferred_element_type=f32)
                ps, alphas = [], []
                for hp in range(PAIR):
                    h = g * A_GROUP + half * PAIR + hp
                    sh = s[hp * TQ:(hp + 1) * TQ]
                    if diag:
                        sh = sh - (2.0 * sig_eff[h]) * ahead
                    sh = jnp.where(sel, sh, NEG_BIG)
                    d = back * sig_eff[h]
                    m_prev = m_scr[h]
                    m_new = jnp.maximum(m_prev, jnp.max(sh, axis=1, keepdims=True) + d)
                    ps.append(jnp.exp2(sh - (m_new[:, 0:1] - d)))
                    alphas.append(jnp.exp2(m_prev - m_new))
                    m_scr[h] = m_new
                p = jnp.concatenate(ps, axis=0).astype(bf16)
                a = jnp.concatenate(alphas, axis=0)
                acc_scr[g, rows, :] = (jnp.concatenate([a, a], axis=1) * acc_scr[g, rows, :]
                                       + jnp.dot(p, vg, preferred_element_type=f32))

    is_diag = kb * TK + TK - 1 > q0

    @pl.when(is_diag)
    def _():
        attend(True, 0)

    @pl.when(jnp.logical_not(is_diag))
    def _():
        back1 = jnp.full((1, LANES), kb * TK - q0, jnp.int32).astype(f32)

        def needed(g):
            worst = None
            for hg in range(A_GROUP):
                h = g * A_GROUP + hg
                over = (qn_scr[h] * (km_ref[g:g + 1, :] * 1.001)
                        + (1.0 + sig_eff[h] * (TK - 1) + UNDERFLOW_LOG2) + back1 * sig_eff[h] - m_scr[h])
                worst = over if worst is None else jnp.maximum(worst, over)
            return jnp.max(worst) > 0.0

        g_first = jnp.int32(A_KV_HEADS - 1)
        for g in range(A_KV_HEADS - 2, -1, -1):
            g_first = jnp.where(needed(g), g, g_first)
        for gs in range(A_KV_HEADS):
            @pl.when(g_first == gs)
            def _():
                attend(False, gs)

    @pl.when(kb == 0)
    def _finish():
        gate = _sigmoid(ga_ref[...] + ba_ref[...])
        for g in range(A_KV_HEADS):
            acc = acc_scr[g]
            og = acc[:, 0:A_DH] / acc[:, A_DH:A_DH + 1]
            for hg in range(A_GROUP):
                cols = slice((g * A_GROUP + hg) * A_DH, (g * A_GROUP + hg + 1) * A_DH)
                ya = og[hg * TQ:(hg + 1) * TQ]
                o_ref[:, cols] = (yr_ref[:, cols] + gate[:, cols] * ya).astype(o_ref.dtype)


def _dsa(g1, g2, ikw, kia, k_all, v_all, yr, b_a, row0, B, T, P, L, TQ, TK, TS):
    Lp = kia.shape[1]
    nkb_total = Lp // TK
    nq = T // TQ
    rb0 = row0 // TQ
    n_sel = min(TOPK_MAX, L // 4)
    steps = []
    for i in range(nq):
        lim_max = min(((P + i * TQ + TQ - 1) // CHUNK + 1) * CHUNK, L)
        steps += [(i, kb) for kb in reversed(range(-(-lim_max // TK)))]
    step_i = jnp.asarray([s[0] for s in steps], jnp.int32)
    step_kb = jnp.asarray([s[1] for s in steps], jnp.int32)
    kx, vx, sig, sig_eff = _alibi_tables(TK)
    kmax = _knorm(k_all, TK)
    rows = lambda b, t, si: rb0 + b * nq + si[t]
    kern = functools.partial(_dsa_kernel, TQ=TQ, TK=TK, TS=TS, L=L, P=P, n_sel=n_sel, sig_eff=sig_eff,
                             idx_bits=int(Lp).bit_length())
    grid_spec = pltpu.PrefetchScalarGridSpec(
        num_scalar_prefetch=2,
        grid=(B, len(steps)),
        in_specs=[pl.BlockSpec((TQ, 2048), lambda b, t, si, sk: (rows(b, t, si), 2)),
                  pl.BlockSpec((TQ, 1024), lambda b, t, si, sk: (rows(b, t, si), 6)),
                  pl.BlockSpec((TQ, LANES), lambda b, t, si, sk: (rows(b, t, si), 0)),
                  pl.BlockSpec((None, Lp, I_DH), lambda b, t, si, sk: (b, 0, 0)),
                  pl.BlockSpec((None, TK, 512), lambda b, t, si, sk: (b, sk[t], 0)),
                  pl.BlockSpec((None, TK, 512), lambda b, t, si, sk: (b, sk[t], 0)),
                  pl.BlockSpec((TK, LANES), lambda b, t, si, sk: (0, 0)),
                  pl.BlockSpec((TK, LANES), lambda b, t, si, sk: (0, 0)),
                  pl.BlockSpec((A_HEADS, LANES), lambda b, t, si, sk: (0, 0)),
                  pl.BlockSpec((None, 8, LANES), lambda b, t, si, sk: (b, sk[t], 0)),
                  pl.BlockSpec((TQ, D_MODEL), lambda b, t, si, sk: (b * nq + si[t], 0)),
                  pl.BlockSpec((TQ, 2048), lambda b, t, si, sk: (rows(b, t, si), 2)),
                  pl.BlockSpec((1, D_MODEL), lambda b, t, si, sk: (0, 0))],
        out_specs=pl.BlockSpec((TQ, D_MODEL), lambda b, t, si, sk: (b * nq + si[t], 0)),
        scratch_shapes=[pltpu.VMEM((I_HEADS * TQ, I_DH), bf16),
                        pltpu.VMEM((A_KV_HEADS, A_GROUP * TQ, 2 * A_DH), bf16),
                        pltpu.VMEM((nkb_total * (TK // TS), TQ, TS), f32),
                        pltpu.VMEM((nkb_total * (TK // TS), TQ, TS), bf16),
                        pltpu.VMEM((TQ, LANES), f32),
                        pltpu.VMEM((TQ, LANES), jnp.int32),
                        pltpu.VMEM((A_HEADS, TQ, LANES), f32),
                        pltpu.VMEM((A_HEADS, TQ, LANES), f32),
                        pltpu.VMEM((A_KV_HEADS, A_GROUP * TQ, 2 * A_DH), f32)])
    return pl.pallas_call(
        kern,
        grid_spec=grid_spec,
        out_shape=jax.ShapeDtypeStruct((B * T, D_MODEL), bf16),
        compiler_params=_cparams(("parallel", "arbitrary")),
        name="dsa",
    )(step_i, step_kb, g1, g1, ikw, kia, k_all, v_all, kx, vx, sig, kmax, yr, g2, b_a)


def _mm_ln_kernel(x_ref, w_ref, r_ref, g_ref, b_ref, o_ref, ob_ref, acc_scr, *, nk):
    kk = pl.program_id(1)

    @pl.when(kk == 0)
    def _():
        acc_scr[...] = jnp.zeros(acc_scr.shape, f32)

    acc_scr[...] += jnp.dot(x_ref[...], w_ref[...], preferred_element_type=f32)

    @pl.when(kk == nk - 1)
    def _():
        y = ALPHA * r_ref[...] + acc_scr[...]
        mu = jnp.mean(y, axis=-1, keepdims=True)
        d = y - mu
        var = jnp.mean(d * d, axis=-1, keepdims=True)
        out = d * lax.rsqrt(var + LN_EPS) * g_ref[...] + b_ref[...]
        o_ref[...] = out
        ob_ref[...] = out.astype(bf16)


def _matmul_ln(x, w, resid, gamma, beta):
    M, K = x.shape
    N = w.shape[1]
    tm = _pick(M, (512, 384, 256, 128, 64))
    tk = _pick(K, (2048, 1408, 512, 256, 128))
    nk = K // tk
    return pl.pallas_call(
        functools.partial(_mm_ln_kernel, nk=nk),
        grid=(M // tm, nk),
        in_specs=[pl.BlockSpec((tm, tk), lambda m, k: (m, k)),
                  pl.BlockSpec((tk, N), lambda m, k: (k, 0)),
                  pl.BlockSpec((tm, N), lambda m, k: (m, 0)),
                  pl.BlockSpec((1, N), lambda m, k: (0, 0)),
                  pl.BlockSpec((1, N), lambda m, k: (0, 0))],
        out_specs=[pl.BlockSpec((tm, N), lambda m, k: (m, 0)),
                   pl.BlockSpec((tm, N), lambda m, k: (m, 0))],
        out_shape=[jax.ShapeDtypeStruct((M, N), f32), jax.ShapeDtypeStruct((M, N), bf16)],
        scratch_shapes=[pltpu.VMEM((tm, N), f32)],
        compiler_params=_cparams(("parallel", "arbitrary")),
        name="proj_ln",
    )(x, w, resid, gamma, beta)


def _gelu_tanh(x):
    return 0.5 * x * (1.0 + jnp.tanh(0.7978845608028654 * (x + 0.044715 * (x * x * x))))


def _up_kernel(x_ref, wh_ref, wu_ref, cw_ref, cb_ref, buf_ref, o_ref, nb_ref, carry_scr, *, tm, n_mt):
    m = pl.program_id(2)

    @pl.when(m == 0)
    def _():
        carry_scr[0:8 - (CONV_W - 1), :] = jnp.zeros((8 - (CONV_W - 1), carry_scr.shape[1]), f32)
        carry_scr[8 - (CONV_W - 1):8, :] = buf_ref[0]

    x = x_ref[...]
    h = jnp.dot(x, wh_ref[...], preferred_element_type=f32)
    u = jnp.dot(x, wu_ref[...], preferred_element_type=f32)
    prev = carry_scr[...]
    row = lax.broadcasted_iota(jnp.int32, h.shape, 0)
    hc = cb_ref[...] + cw_ref[CONV_W - 1:CONV_W, :] * h
    for s in range(1, CONV_W):
        shifted = pltpu.roll(h, s, 0)
        head = pltpu.roll(prev, s, 0)
        head = jnp.concatenate([head] * (tm // 8), axis=0)
        shifted = jnp.where(row < s, head, shifted)
        hc = hc + cw_ref[CONV_W - 1 - s:CONV_W - s, :] * shifted
    o_ref[...] = (_gelu_tanh(hc) * u).astype(o_ref.dtype)
    carry_scr[...] = h[tm - 8:tm, :]

    @pl.when(m == n_mt - 1)
    def _():
        nb_ref[0] = h[tm - (CONV_W - 1):tm, :]


def _up_conv(x, w_h, w_u, conv_w, conv_b, buf, row0, B, T):
    K = x.shape[1]
    tm = _pick(T, (1024, 512, 256, 128, 64))
    tn = 512
    n_mt = T // tm
    rb0 = row0 // tm
    return pl.pallas_call(
        functools.partial(_up_kernel, tm=tm, n_mt=n_mt),
        grid=(D_FF // tn, B, n_mt),
        in_specs=[pl.BlockSpec((tm, K), lambda n, b, m: (rb0 + b * n_mt + m, 0)),
                  pl.BlockSpec((K, tn), lambda n, b, m: (0, n)),
                  pl.BlockSpec((K, tn), lambda n, b, m: (0, n)),
                  pl.BlockSpec((CONV_W, tn), lambda n, b, m: (0, n)),
                  pl.BlockSpec((1, tn), lambda n, b, m: (0, n)),
                  pl.BlockSpec((1, CONV_W - 1, tn), lambda n, b, m: (b, 0, n))],
        out_specs=[pl.BlockSpec((tm, tn), lambda n, b, m: (b * n_mt + m, n)),
                   pl.BlockSpec((1, CONV_W - 1, tn), lambda n, b, m: (b, 0, n))],
        out_shape=[jax.ShapeDtypeStruct((B * T, D_FF), bf16),
                   jax.ShapeDtypeStruct((B, CONV_W - 1, D_FF), f32)],
        scratch_shapes=[pltpu.VMEM((8, tn), f32)],
        compiler_params=_cparams(("parallel", "arbitrary", "arbitrary")),
        name="up_conv",
    )(x, w_h, w_u, conv_w, conv_b, buf)


def _pad_keys(a, lp):
    return jnp.pad(a, ((0, 0), (0, lp - a.shape[1]), (0, 0)))


def kernel(x_prompt, x_sample, cache_attn_k, cache_attn_v, cache_idx_k, state_ret, state_conv,
           w_in, b_gate, gn_g, w_out, ln1_g, ln1_b, w_up, conv_w, conv_b, w_down, ln2_g, ln2_b):
    Bp, Tp, _ = x_prompt.shape
    Bs, Ts, _ = x_sample.shape
    P = cache_attn_k.shape[2]
    Mp, Ms = Bp * Tp, Bs * Ts
    M = Mp + Ms
    x_all = jnp.concatenate([x_prompt.reshape(Mp, D_MODEL), x_sample.reshape(Ms, D_MODEL)], axis=0)
    xb = x_all.astype(bf16)

    splits = (R_HEADS * R_DK, R_HEADS * R_DK, D_MODEL, D_MODEL, A_HEADS * A_DH, A_KV_HEADS * A_DH,
              A_KV_HEADS * A_DH, I_HEADS * I_DH, I_DH, I_HEADS, D_MODEL, D_MODEL)
    off = np.concatenate([[0], np.cumsum(splits)])
    wi = w_in[0]
    col = lambda i: wi[:, off[i]:off[i + 1]]
    rq, rk, rv, rg, aq, ak, av, iq, ik, iw, g_r, g_a = range(12)
    w1 = jnp.concatenate([col(rq), col(rk), col(rv), col(aq), col(iq)], axis=1).astype(bf16)
    s1 = jnp.concatenate([jnp.ones((1024,), f32), jnp.full((1024,), R_DK ** -0.5, f32),
                          jnp.ones((2048,), f32), jnp.full((2048,), A_DH ** -0.5 * LOG2E, f32),
                          jnp.ones((1024,), f32)])[None]
    w2 = jnp.concatenate([col(rg), col(g_r), col(g_a)], axis=1).astype(bf16)
    w3 = jnp.concatenate([col(ak), col(av)], axis=1).astype(bf16)
    w4 = jnp.pad(jnp.concatenate([col(ik), col(iw)], axis=1),
                 ((0, 0), (0, LANES - I_DH - I_HEADS))).astype(bf16)
    g1 = _matmul(xb, w1, s1, bf16, 1024)
    g2 = _matmul(xb, w2, jnp.ones((1, w2.shape[1]), f32), f32, 1024)
    kv = _matmul(xb, w3, jnp.ones((1, w3.shape[1]), f32), f32, 512)
    ikw = _matmul(xb, w4, jnp.ones((1, LANES), f32), f32, LANES)

    b_r = b_gate[0, :D_MODEL][None]
    b_a = b_gate[0, D_MODEL:][None]
    gn = gn_g[0][None]

    yr_p, sn_p = _retention(g1, g2, gn, b_r, jnp.zeros((Bp, R_HEADS, R_DK, R_DV), f32), 0, Bp, Tp)
    yr_s, sn_s = _retention(g1, g2, gn, b_r, state_ret[0].astype(f32), Mp, Bs, Ts)

    kvb = kv.astype(bf16)
    ikb = ikw[:, :I_DH].astype(bf16)
    tq_p = _pick(Tp, (128,))
    tk_p = _pick(Tp, (1024, 512, 256, 128))
    ts_p = min(tk_p, 1024)
    mix_p = _dsa(g1, g2, ikw, ikb[:Mp].reshape(Bp, Tp, I_DH), kvb[:Mp, :512].reshape(Bp, Tp, 512),
                 kvb[:Mp, 512:].reshape(Bp, Tp, 512), yr_p, b_a, 0, Bp, Tp, 0, Tp, tq_p, tk_p, ts_p)
    Ls = P + Ts
    lp = -(-Ls // LANES) * LANES
    ki_s = jnp.concatenate([cache_idx_k[0].astype(bf16), ikb[Mp:].reshape(Bs, Ts, I_DH)], axis=1)
    k_s = jnp.concatenate([cache_attn_k[0].reshape(Bs, P, 512).astype(bf16),
                           kvb[Mp:, :512].reshape(Bs, Ts, 512)], axis=1)
    v_s = jnp.concatenate([cache_attn_v[0].reshape(Bs, P, 512).astype(bf16),
                           kvb[Mp:, 512:].reshape(Bs, Ts, 512)], axis=1)
    mix_s = _dsa(g1, g2, ikw, _pad_keys(ki_s, lp), _pad_keys(k_s, lp), _pad_keys(v_s, lp), yr_s, b_a,
                 Mp, Bs, Ts, P, Ls, Ts, lp, lp)

    wo, wd = w_out[0].astype(bf16), w_down[0].astype(bf16)
    wu = w_up[0].astype(bf16)
    w_h, w_u = wu[:, :D_FF], wu[:, D_FF:]
    g_1, b_1, g_2, b_2 = ln1_g[0][None], ln1_b[0][None], ln2_g[0][None], ln2_b[0][None]
    x1_p, x1b_p = _matmul_ln(mix_p, wo, x_prompt.reshape(Mp, D_MODEL), g_1, b_1)
    x1_s, x1b_s = _matmul_ln(mix_s, wo, x_sample.reshape(Ms, D_MODEL), g_1, b_1)
    ff_p, nb_p = _up_conv(x1b_p, w_h, w_u, conv_w[0], conv_b[0][None],
                          jnp.zeros((Bp, CONV_W - 1, D_FF), f32), 0, Bp, Tp)
    ff_s, nb_s = _up_conv(x1b_s, w_h, w_u, conv_w[0], conv_b[0][None], state_conv[0].astype(f32), 0, Bs, Ts)
    y_p, _ = _matmul_ln(ff_p, wd, x1_p, g_2, b_2)
    y_s, _ = _matmul_ln(ff_s, wd, x1_s, g_2, b_2)

    dt = x_prompt.dtype
    return (y_p.reshape(Bp, Tp, D_MODEL).astype(dt),
            y_s.reshape(Bs, Ts, D_MODEL).astype(dt),
            kv[:Mp, :512].reshape(1, Bp, Tp, A_KV_HEADS, A_DH).astype(dt),
            kv[:Mp, 512:].reshape(1, Bp, Tp, A_KV_HEADS, A_DH).astype(dt),
            ikw[:Mp, :I_DH].reshape(1, Bp, Tp, I_DH).astype(dt),
            sn_p[None].astype(dt),
            nb_p[None].astype(dt),
            kv[Mp:, :512].reshape(1, Bs, Ts, A_KV_HEADS, A_DH).astype(dt),
            kv[Mp:, 512:].reshape(1, Bs, Ts, A_KV_HEADS, A_DH).astype(dt),
            ikw[Mp:, :I_DH].reshape(1, Bs, Ts, I_DH).astype(dt),
            sn_s[None].astype(dt),
            nb_s[None].astype(dt))
```

```python
import functools

import numpy as np
import jax
import jax.numpy as jnp
from jax import lax
from jax.experimental import pallas as pl
from jax.experimental.pallas import tpu as pltpu

f32 = jnp.float32
bf16 = jnp.bfloat16

D_MODEL = 2048
CHUNK = 64
R_HEADS = 8
R_DK = 128
R_DV = 256
A_HEADS = 16
A_KV_HEADS = 4
A_DH = 128
A_GROUP = 4
I_HEADS = 16
I_DH = 64
TOPK_MAX = 256
D_FF = 5632
CONV_W = 3
DEPTH = 1
ALPHA = (2 * DEPTH) ** 0.25
LN_EPS = 1e-5

LANES = 128
VMEM_LIMIT = 56 * 1024 * 1024
NEG_BIG = -1e30
F32_LOWEST = -3.4028234663852886e38
INT_MIN = -2 ** 31
LOG2E = 1.4426950408889634
JCUT_ALL = 2 ** 30
UNDERFLOW_LOG2 = 160.0


def _pick(n, cands):
    for c in cands:
        if n % c == 0:
            return c
    raise ValueError(f"no tile in {cands} divides {n}")


def _cparams(sem):
    return pltpu.CompilerParams(dimension_semantics=sem, vmem_limit_bytes=VMEM_LIMIT)


def _sigmoid(x):
    return 1.0 / (1.0 + jnp.exp(-x))


def _mm_kernel(x_ref, w_ref, s_ref, o_ref):
    acc = jnp.dot(x_ref[...], w_ref[...], preferred_element_type=f32)
    o_ref[...] = (acc * s_ref[...]).astype(o_ref.dtype)


def _matmul(x, w, scale, out_dtype, tn):
    M, K = x.shape
    N = w.shape[1]
    tm = _pick(M, (1536, 1024, 512, 384, 256, 128, 64))
    return pl.pallas_call(
        _mm_kernel,
        grid=(N // tn, M // tm),
        in_specs=[pl.BlockSpec((tm, K), lambda n, m: (m, 0)),
                  pl.BlockSpec((K, tn), lambda n, m: (0, n)),
                  pl.BlockSpec((1, tn), lambda n, m: (0, n))],
        out_specs=pl.BlockSpec((tm, tn), lambda n, m: (m, n)),
        out_shape=jax.ShapeDtypeStruct((M, N), out_dtype),
        compiler_params=_cparams(("parallel", "parallel")),
        name="in_proj",
    )(x, w, scale)


def _ret_kernel(q_ref, k_ref, v_ref, rg_ref, gr_ref, dm_ref, cd_ref, kd_ref, gc_ref, gn_ref, br_ref,
                s0_ref, y_ref, sn_ref, s_scr, *, n_chunks):
    c = pl.program_id(1)

    @pl.when(c == 0)
    def _():
        s_scr[...] = s0_ref[0]

    for h in range(R_HEADS):
        q = q_ref[:, h * R_DK:(h + 1) * R_DK]
        k = k_ref[:, h * R_DK:(h + 1) * R_DK]
        v = v_ref[:, h * R_DV:(h + 1) * R_DV]
        s_old = s_scr[h]
        sc = lax.dot_general(q, k, (((1,), (1,)), ((), ())), preferred_element_type=f32) * dm_ref[h]
        inner = jnp.dot(sc.astype(bf16), v, preferred_element_type=f32)
        cross = jnp.dot(q, s_old.astype(bf16), preferred_element_type=f32) * cd_ref[h]
        o = inner + cross
        kdk = (k.astype(f32) * kd_ref[h]).astype(bf16)
        upd = lax.dot_general(kdk, v, (((0,), (0,)), ((), ())), preferred_element_type=f32)
        s_scr[h] = gc_ref[h] * s_old + upd
        mu = jnp.mean(o, axis=-1, keepdims=True)
        d = o - mu
        var = jnp.mean(d * d, axis=-1, keepdims=True)
        cols = slice(h * R_DV, (h + 1) * R_DV)
        on = d * lax.rsqrt(var + LN_EPS) * gn_ref[:, cols]
        rg = rg_ref[:, cols]
        gate = _sigmoid(gr_ref[:, cols] + br_ref[:, cols])
        y_ref[:, cols] = gate * (rg * _sigmoid(rg) * on)

    @pl.when(c == n_chunks - 1)
    def _():
        sn_ref[0] = s_scr[...]


def _retention_tables(C):
    lg = np.log1p(-np.exp2(-5.0 - np.arange(R_HEADS, dtype=np.float64)))
    i = np.arange(C, dtype=np.float64)
    diff = i[:, None] - i[None, :]
    dm = np.where(diff >= 0, np.exp(lg[:, None, None] * np.maximum(diff, 0.0)), 0.0)
    cd = np.exp((i + 1.0)[None, :] * lg[:, None])
    kd = np.exp((C - 1.0 - i)[None, :] * lg[:, None])
    gc = np.exp(C * lg)
    return (jnp.asarray(dm, f32),
            jnp.asarray(np.broadcast_to(cd[:, :, None], (R_HEADS, C, R_DV)), f32),
            jnp.asarray(np.broadcast_to(kd[:, :, None], (R_HEADS, C, R_DK)), f32),
            jnp.asarray(np.broadcast_to(gc[:, None, None], (R_HEADS, 1, R_DV)), f32))


def _retention(g1, g2, gn_g, b_r, s0, row0, B, T):
    C = min(T, 256)
    n_chunks = T // C
    rb0 = row0 // C
    dm, cd, kd, gc = _retention_tables(C)
    rows = lambda b, c: rb0 + b * n_chunks + c
    const3 = lambda b, c: (0, 0, 0)
    return pl.pallas_call(
        functools.partial(_ret_kernel, n_chunks=n_chunks),
        grid=(B, n_chunks),
        in_specs=[pl.BlockSpec((C, 1024), lambda b, c: (rows(b, c), 0)),
                  pl.BlockSpec((C, 1024), lambda b, c: (rows(b, c), 1)),
                  pl.BlockSpec((C, 2048), lambda b, c: (rows(b, c), 1)),
                  pl.BlockSpec((C, 2048), lambda b, c: (rows(b, c), 0)),
                  pl.BlockSpec((C, 2048), lambda b, c: (rows(b, c), 1)),
                  pl.BlockSpec((R_HEADS, C, C), const3),
                  pl.BlockSpec((R_HEADS, C, R_DV), const3),
                  pl.BlockSpec((R_HEADS, C, R_DK), const3),
                  pl.BlockSpec((R_HEADS, 1, R_DV), const3),
                  pl.BlockSpec((1, D_MODEL), lambda b, c: (0, 0)),
                  pl.BlockSpec((1, D_MODEL), lambda b, c: (0, 0)),
                  pl.BlockSpec((1, R_HEADS, R_DK, R_DV), lambda b, c: (b, 0, 0, 0))],
        out_specs=[pl.BlockSpec((C, D_MODEL), lambda b, c: (b * n_chunks + c, 0)),
                   pl.BlockSpec((1, R_HEADS, R_DK, R_DV), lambda b, c: (b, 0, 0, 0))],
        out_shape=[jax.ShapeDtypeStruct((B * T, D_MODEL), f32),
                   jax.ShapeDtypeStruct((B, R_HEADS, R_DK, R_DV), f32)],
        scratch_shapes=[pltpu.VMEM((R_HEADS, R_DK, R_DV), f32)],
        compiler_params=_cparams(("parallel", "arbitrary")),
        name="retention",
    )(g1, g1, g1, g2, g2, dm, cd, kd, gc, gn_g, b_r, s0)


def _bf16_split3(x):
    parts = []
    rem = np.float64(x)
    for _ in range(3):
        p = np.float64(np.asarray(rem, np.float32).astype(jnp.bfloat16).astype(np.float32))
        parts.append(p)
        rem = rem - p
    return parts


def _alibi_tables(TK):
    r = 1
    while TK // r > 256:
        r *= 2
    jj = np.arange(TK)
    kx = np.zeros((TK, LANES), np.float32)
    kx[:, 0:3] = (jj // r)[:, None]
    kx[:, 3:6] = (jj % r)[:, None]
    sig = np.zeros((A_HEADS, LANES), np.float32)
    sig_eff = []
    for h in range(A_HEADS):
        s3 = _bf16_split3(2.0 ** (-8.0 * (h + 1) / A_HEADS) * LOG2E)
        sig[h, 0:3] = [r * p for p in s3]
        sig[h, 3:6] = s3
        sig_eff.append(float(sum(s3)))
    vx = np.zeros((TK, LANES), np.float32)
    vx[:, 0] = 1.0
    return jnp.asarray(kx, bf16), jnp.asarray(vx, bf16), jnp.asarray(sig, f32), tuple(sig_eff)


def _key_to_float(u):
    key = u ^ jnp.int32(INT_MIN)
    bits = key ^ ((key >> 31) & jnp.int32(0x7FFFFFFF))
    return lax.bitcast_convert_type(bits, f32)


def _upper_half(x):
    bits = lax.bitcast_convert_type(x, jnp.int32) & jnp.int32(-65536)
    return lax.bitcast_convert_type(bits, f32)


def _knorm_kernel(k_ref, o_ref):
    rows = []
    for g in range(A_KV_HEADS):
        kg = k_ref[:, g * A_DH:(g + 1) * A_DH].astype(f32)
        n2 = jnp.max(jnp.sum(kg * kg, axis=1, keepdims=True), axis=0, keepdims=True)
        rows.append(jnp.broadcast_to(jnp.sqrt(n2), (1, LANES)))
    rows.append(jnp.zeros((8 - A_KV_HEADS, LANES), f32))
    o_ref[...] = jnp.concatenate(rows, axis=0)


def _knorm(k_all, TK):
    B, Lp, _ = k_all.shape
    nkb = Lp // TK
    return pl.pallas_call(
        _knorm_kernel,
        grid=(B, nkb),
        in_specs=[pl.BlockSpec((None, TK, 512), lambda b, kb: (b, kb, 0))],
        out_specs=pl.BlockSpec((None, 8, LANES), lambda b, kb: (b, kb, 0)),
        out_shape=jax.ShapeDtypeStruct((B, nkb * 8, LANES), f32),
        compiler_params=_cparams(("parallel", "parallel")),
        name="knorm",
    )(k_all)


def _dsa_kernel(si_ref, skb_ref, aq_ref, iq_ref, ikw_ref, kia_ref, k_ref, v_ref, kx_ref, vx_ref, sig_ref,
                km_ref, yr_ref, ga_ref, ba_ref, o_ref,
                qi_scr, qg_scr, isc_scr, hi_scr, thr_scr, jcut_scr, qn_scr, m_scr, acc_scr,
                *, TQ, TK, TS, L, P, n_sel, sig_eff, idx_bits):
    t = pl.program_id(1)
    i = si_ref[t]
    kb = skb_ref[t]
    r = TK // TS
    q0 = P + i * TQ
    lim_max = jnp.minimum(((q0 + TQ - 1) // CHUNK + 1) * CHUNK, L)
    nkb = (lim_max + TK - 1) // TK
    nsb = (lim_max + TS - 1) // TS
    pos = q0 + lax.broadcasted_iota(jnp.int32, (TQ, 1), 0)
    lim = jnp.minimum((lax.shift_right_logical(pos, 6) + 1) * CHUNK, L)

    def count_rows(pred):
        lane = lax.broadcasted_iota(jnp.int32, (TQ, LANES), 1)

        def block(sb, cnt):
            x = isc_scr[sb]
            for c in range(TS // LANES):
                hit = pred(x[:, c * LANES:(c + 1) * LANES], sb * TS + c * LANES + lane)
                cnt = cnt + jnp.where(hit, 1.0, 0.0)
            return cnt

        cnt = lax.fori_loop(0, nsb, block, jnp.zeros((TQ, LANES), f32))
        return jnp.sum(cnt, axis=1, keepdims=True)

    @pl.when(kb == nkb - 1)
    def _prepare():
        for h in range(I_HEADS):
            qi_scr[h * TQ:(h + 1) * TQ, :] = iq_ref[:, h * I_DH:(h + 1) * I_DH]
        for g in range(A_KV_HEADS):
            for hg in range(A_GROUP):
                h = g * A_GROUP + hg
                qh = aq_ref[:, h * A_DH:(h + 1) * A_DH]
                qg_scr[g, hg * TQ:(hg + 1) * TQ, 0:A_DH] = qh
                qg_scr[g, hg * TQ:(hg + 1) * TQ, A_DH:2 * A_DH] = jnp.broadcast_to(
                    sig_ref[h:h + 1, :], (TQ, LANES)).astype(bf16)
                qf = qh.astype(f32)
                qn_scr[h] = jnp.broadcast_to(jnp.sqrt(jnp.sum(qf * qf, axis=1, keepdims=True)), (TQ, LANES))
        w = ikw_ref[:, I_DH:I_DH + I_HEADS] * (I_DH ** -0.5 * I_HEADS ** -0.5)

        def score_block(sb, carry):
            start = pl.multiple_of(sb * TS, TS)
            ki = kia_ref[pl.ds(start, TS), :]
            logits = lax.dot_general(qi_scr[...], ki, (((1,), (1,)), ((), ())),
                                     preferred_element_type=f32)
            isc = jnp.zeros((TQ, TS), f32)
            for h in range(I_HEADS):
                isc = isc + jnp.maximum(logits[h * TQ:(h + 1) * TQ], 0.0) * w[:, h:h + 1]
            j = start + lax.broadcasted_iota(jnp.int32, (TQ, TS), 1)
            isc = jnp.where(j < lim, isc, -jnp.inf)
            isc_scr[sb] = isc
            hi_scr[sb] = _upper_half(isc).astype(bf16)
            return carry

        lax.fori_loop(0, nsb, score_block, 0)

        def fill_block(sb, carry):
            isc_scr[sb] = jnp.full((TQ, TS), -jnp.inf, f32)
            return carry

        lax.fori_loop(nsb, nkb * r, fill_block, 0)

        def hi_pass(b, carry):
            u, cu = carry
            cand = u | lax.shift_left(jnp.int32(1), 31 - b)
            tb = jnp.broadcast_to(_upper_half(_key_to_float(cand)), (TQ, LANES)).astype(bf16)

            def block(sb, cnt):
                x = hi_scr[sb]
                for c in range(TS // LANES):
                    cnt = cnt + jnp.where(x[:, c * LANES:(c + 1) * LANES] >= tb, one_b, zero_b)
                return cnt

            one_b, zero_b = jnp.ones((), bf16), jnp.zeros((), bf16)
            cnt = lax.fori_loop(0, nsb, block, jnp.zeros((TQ, LANES), bf16))
            total = jnp.sum(cnt.astype(f32), axis=1, keepdims=True)
            ok = total >= float(n_sel)
            return jnp.where(ok, cand, u), jnp.where(ok, total, cu)

        def bit_pass(b, carry):
            u, cu = carry
            cand = u | lax.shift_left(jnp.int32(1), 31 - b)
            tf = _key_to_float(cand)
            total = count_rows(lambda x, j: x >= tf)
            ok = total >= float(n_sel)
            return jnp.where(ok, cand, u), jnp.where(ok, total, cu)

        carry = lax.fori_loop(0, 16, hi_pass, (jnp.zeros((TQ, 1), jnp.int32), jnp.zeros((TQ, 1), f32)))
        u, cu = lax.fori_loop(16, 32, bit_pass, carry)
        thr = jnp.where(lim <= n_sel, F32_LOWEST, _key_to_float(u))
        thr_scr[...] = jnp.broadcast_to(thr, (TQ, LANES))
        jcut_scr[...] = jnp.full((TQ, LANES), JCUT_ALL, jnp.int32)
        m_scr[...] = jnp.full(m_scr.shape, NEG_BIG, f32)
        acc_scr[...] = jnp.zeros(acc_scr.shape, f32)

        tied = jnp.logical_and(lim > n_sel, cu > float(n_sel))

        @pl.when(jnp.max(jnp.where(tied, 1.0, 0.0)) > 0.0)
        def _ties():
            room = float(n_sel) - count_rows(lambda x, j: x > thr)

            def idx_pass(b, jc):
                cand = jc | lax.shift_left(jnp.int32(1), idx_bits - 1 - b)
                c = count_rows(lambda x, j: jnp.logical_and(x == thr, j < cand))
                return jnp.where(c <= room, cand, jc)

            jc = lax.fori_loop(0, idx_bits, idx_pass, jnp.zeros((TQ, 1), jnp.int32))
            jcut_scr[...] = jnp.broadcast_to(jnp.where(tied, jc, JCUT_ALL), (TQ, LANES))

    def attend(diag, g_first):
        x = jnp.concatenate([isc_scr[kb * r + c] for c in range(r)], axis=1)
        thr = thr_scr[:, 0:1]
        j = kb * TK + lax.broadcasted_iota(jnp.int32, (TQ, TK), 1)
        sel = jnp.logical_or(x > thr, jnp.logical_and(x == thr, j < jcut_scr[:, 0:1]))
        if diag:
            ahead = jnp.maximum(j - pos, 0).astype(f32)
        back = jnp.full((TQ, 1), kb * TK - q0, jnp.int32).astype(f32)
        PAIR = A_GROUP // 2
        for g in range(g_first, A_KV_HEADS):
            kg = jnp.concatenate([k_ref[:, g * A_DH:(g + 1) * A_DH], kx_ref[...]], axis=1)
            vg = jnp.concatenate([v_ref[:, g * A_DH:(g + 1) * A_DH], vx_ref[...]], axis=1)
            for half in range(A_GROUP // PAIR):
                rows = slice(half * PAIR * TQ, (half + 1) * PAIR * TQ)
                s = lax.dot_general(qg_scr[g, rows, :], kg, (((1,), (1,)), ((), ())),
                                    preferred_element_type=f32)
                ps, alphas = [], []
                for hp in range(PAIR):
                    h = g * A_GROUP + half * PAIR + hp
                    sh = s[hp * TQ:(hp + 1) * TQ]
                    if diag:
                        sh = sh - (2.0 * sig_eff[h]) * ahead
                    sh = jnp.where(sel, sh, NEG_BIG)
                    d = back * sig_eff[h]
                    m_prev = m_scr[h]
                    m_new = jnp.maximum(m_prev, jnp.max(sh, axis=1, keepdims=True) + d)
                    ps.append(jnp.exp2(sh - (m_new[:, 0:1] - d)))
                    alphas.append(jnp.exp2(m_prev - m_new))
                    m_scr[h] = m_new
                p = jnp.concatenate(ps, axis=0).astype(bf16)
                a = jnp.concatenate(alphas, axis=0)
                acc_scr[g, rows, :] = (jnp.concatenate([a, a], axis=1) * acc_scr[g, rows, :]
                                       + jnp.dot(p, vg, preferred_element_type=f32))

    is_diag = kb * TK + TK - 1 > q0

    @pl.when(is_diag)
    def _():
        attend(True, 0)

    @pl.when(jnp.logical_not(is_diag))
    def _():
        back1 = jnp.full((1, LANES), kb * TK - q0, jnp.int32).astype(f32)

        def needed(g):
            worst = None
            for hg in range(A_GROUP):
                h = g * A_GROUP + hg
                over = (qn_scr[h] * (km_ref[g:g + 1, :] * 1.001)
                        + (1.0 + sig_eff[h] * (TK - 1) + UNDERFLOW_LOG2) + back1 * sig_eff[h] - m_scr[h])
                worst = over if worst is None else jnp.maximum(worst, over)
            return jnp.max(worst) > 0.0

        g_first = jnp.int32(A_KV_HEADS - 1)
        for g in range(A_KV_HEADS - 2, -1, -1):
            g_first = jnp.where(needed(g), g, g_first)
        for gs in range(A_KV_HEADS):
            @pl.when(g_first == gs)
            def _():
                attend(False, gs)

    @pl.when(kb == 0)
    def _finish():
        gate = _sigmoid(ga_ref[...] + ba_ref[...])
        for g in range(A_KV_HEADS):
            acc = acc_scr[g]
            og = acc[:, 0:A_DH] / acc[:, A_DH:A_DH + 1]
            for hg in range(A_GROUP):
                cols = slice((g * A_GROUP + hg) * A_DH, (g * A_GROUP + hg + 1) * A_DH)
                ya = og[hg * TQ:(hg + 1) * TQ]
                o_ref[:, cols] = (yr_ref[:, cols] + gate[:, cols] * ya).astype(o_ref.dtype)


def _dsa(g1, g2, ikw, kia, k_all, v_all, yr, b_a, row0, B, T, P, L, TQ, TK, TS):
    Lp = kia.shape[1]
    nkb_total = Lp // TK
    nq = T // TQ
    rb0 = row0 // TQ
    n_sel = min(TOPK_MAX, L // 4)
    steps = []
    for i in range(nq):
        lim_max = min(((P + i * TQ + TQ - 1) // CHUNK + 1) * CHUNK, L)
        steps += [(i, kb) for kb in reversed(range(-(-lim_max // TK)))]
    step_i = jnp.asarray([s[0] for s in steps], jnp.int32)
    step_kb = jnp.asarray([s[1] for s in steps], jnp.int32)
    kx, vx, sig, sig_eff = _alibi_tables(TK)
    kmax = _knorm(k_all, TK)
    rows = lambda b, t, si: rb0 + b * nq + si[t]
    kern = functools.partial(_dsa_kernel, TQ=TQ, TK=TK, TS=TS, L=L, P=P, n_sel=n_sel, sig_eff=sig_eff,
                             idx_bits=int(Lp).bit_length())
    grid_spec = pltpu.PrefetchScalarGridSpec(
        num_scalar_prefetch=2,
        grid=(B, len(steps)),
        in_specs=[pl.BlockSpec((TQ, 2048), lambda b, t, si, sk: (rows(b, t, si), 2)),
                  pl.BlockSpec((TQ, 1024), lambda b, t, si, sk: (rows(b, t, si), 6)),
                  pl.BlockSpec((TQ, LANES), lambda b, t, si, sk: (rows(b, t, si), 0)),
                  pl.BlockSpec((None, Lp, I_DH), lambda b, t, si, sk: (b, 0, 0)),
                  pl.BlockSpec((None, TK, 512), lambda b, t, si, sk: (b, sk[t], 0)),
                  pl.BlockSpec((None, TK, 512), lambda b, t, si, sk: (b, sk[t], 0)),
                  pl.BlockSpec((TK, LANES), lambda b, t, si, sk: (0, 0)),
                  pl.BlockSpec((TK, LANES), lambda b, t, si, sk: (0, 0)),
                  pl.BlockSpec((A_HEADS, LANES), lambda b, t, si, sk: (0, 0)),
                  pl.BlockSpec((None, 8, LANES), lambda b, t, si, sk: (b, sk[t], 0)),
                  pl.BlockSpec((TQ, D_MODEL), lambda b, t, si, sk: (b * nq + si[t], 0)),
                  pl.BlockSpec((TQ, 2048), lambda b, t, si, sk: (rows(b, t, si), 2)),
                  pl.BlockSpec((1, D_MODEL), lambda b, t, si, sk: (0, 0))],
        out_specs=pl.BlockSpec((TQ, D_MODEL), lambda b, t, si, sk: (b * nq + si[t], 0)),
        scratch_shapes=[pltpu.VMEM((I_HEADS * TQ, I_DH), bf16),
                        pltpu.VMEM((A_KV_HEADS, A_GROUP * TQ, 2 * A_DH), bf16),
                        pltpu.VMEM((nkb_total * (TK // TS), TQ, TS), f32),
                        pltpu.VMEM((nkb_total * (TK // TS), TQ, TS), bf16),
                        pltpu.VMEM((TQ, LANES), f32),
                        pltpu.VMEM((TQ, LANES), jnp.int32),
                        pltpu.VMEM((A_HEADS, TQ, LANES), f32),
                        pltpu.VMEM((A_HEADS, TQ, LANES), f32),
                        pltpu.VMEM((A_KV_HEADS, A_GROUP * TQ, 2 * A_DH), f32)])
    return pl.pallas_call(
        kern,
        grid_spec=grid_spec,
        out_shape=jax.ShapeDtypeStruct((B * T, D_MODEL), bf16),
        compiler_params=_cparams(("parallel", "arbitrary")),
        name="dsa",
    )(step_i, step_kb, g1, g1, ikw, kia, k_all, v_all, kx, vx, sig, kmax, yr, g2, b_a)


def _mm_ln_kernel(x_ref, w_ref, r_ref, g_ref, b_ref, o_ref, ob_ref, acc_scr, *, nk):
    kk = pl.program_id(1)

    @pl.when(kk == 0)
    def _():
        acc_scr[...] = jnp.zeros(acc_scr.shape, f32)

    acc_scr[...] += jnp.dot(x_ref[...], w_ref[...], preferred_element_type=f32)

    @pl.when(kk == nk - 1)
    def _():
        y = ALPHA * r_ref[...] + acc_scr[...]
        mu = jnp.mean(y, axis=-1, keepdims=True)
        d = y - mu
        var = jnp.mean(d * d, axis=-1, keepdims=True)
        out = d * lax.rsqrt(var + LN_EPS) * g_ref[...] + b_ref[...]
        o_ref[...] = out
        ob_ref[...] = out.astype(bf16)


def _matmul_ln(x, w, resid, gamma, beta):
    M, K = x.shape
    N = w.shape[1]
    tm = _pick(M, (512, 384, 256, 128, 64))
    tk = _pick(K, (2048, 1408, 512, 256, 128))
    nk = K // tk
    return pl.pallas_call(
        functools.partial(_mm_ln_kernel, nk=nk),
        grid=(M // tm, nk),
        in_specs=[pl.BlockSpec((tm, tk), lambda m, k: (m, k)),
                  pl.BlockSpec((tk, N), lambda m, k: (k, 0)),
                  pl.BlockSpec((tm, N), lambda m, k: (m, 0)),
                  pl.BlockSpec((1, N), lambda m, k: (0, 0)),
                  pl.BlockSpec((1, N), lambda m, k: (0, 0))],
        out_specs=[pl.BlockSpec((tm, N), lambda m, k: (m, 0)),
                   pl.BlockSpec((tm, N), lambda m, k: (m, 0))],
        out_shape=[jax.ShapeDtypeStruct((M, N), f32), jax.ShapeDtypeStruct((M, N), bf16)],
        scratch_shapes=[pltpu.VMEM((tm, N), f32)],
        compiler_params=_cparams(("parallel", "arbitrary")),
        name="proj_ln",
    )(x, w, resid, gamma, beta)


def _gelu_tanh(x):
    return 0.5 * x * (1.0 + jnp.tanh(0.7978845608028654 * (x + 0.044715 * (x * x * x))))


def _up_kernel(x_ref, wh_ref, wu_ref, cw_ref, cb_ref, buf_ref, o_ref, nb_ref, carry_scr, wh_scr, wu_scr,
               *, tm, n_mt):
    m = pl.program_id(2)

    @pl.when(jnp.logical_and(pl.program_id(1) == 0, m == 0))
    def _():
        wh_scr[...] = wh_ref[...].astype(bf16)
        wu_scr[...] = wu_ref[...].astype(bf16)

    @pl.when(m == 0)
    def _():
        carry_scr[0:8 - (CONV_W - 1), :] = jnp.zeros((8 - (CONV_W - 1), carry_scr.shape[1]), f32)
        carry_scr[8 - (CONV_W - 1):8, :] = buf_ref[0]

    x = x_ref[...]
    h = jnp.dot(x, wh_scr[...], preferred_element_type=f32)
    u = jnp.dot(x, wu_scr[...], preferred_element_type=f32)
    prev = carry_scr[...]
    row = lax.broadcasted_iota(jnp.int32, h.shape, 0)
    hc = cb_ref[...] + cw_ref[CONV_W - 1:CONV_W, :] * h
    for s in range(1, CONV_W):
        shifted = pltpu.roll(h, s, 0)
        head = pltpu.roll(prev, s, 0)
        head = jnp.concatenate([head] * (tm // 8), axis=0)
        shifted = jnp.where(row < s, head, shifted)
        hc = hc + cw_ref[CONV_W - 1 - s:CONV_W - s, :] * shifted
    o_ref[...] = (_gelu_tanh(hc) * u).astype(o_ref.dtype)
    carry_scr[...] = h[tm - 8:tm, :]

    @pl.when(m == n_mt - 1)
    def _():
        nb_ref[0] = h[tm - (CONV_W - 1):tm, :]


def _up_conv(x, w_up, conv_w, conv_b, buf, row0, B, T):
    K = x.shape[1]
    tm = _pick(T, (1024, 512, 256, 128, 64))
    tn = 512
    n_mt = T // tm
    rb0 = row0 // tm
    return pl.pallas_call(
        functools.partial(_up_kernel, tm=tm, n_mt=n_mt),
        grid=(D_FF // tn, B, n_mt),
        in_specs=[pl.BlockSpec((tm, K), lambda n, b, m: (rb0 + b * n_mt + m, 0)),
                  pl.BlockSpec((K, tn), lambda n, b, m: (0, n)),
                  pl.BlockSpec((K, tn), lambda n, b, m: (0, n + D_FF // tn)),
                  pl.BlockSpec((CONV_W, tn), lambda n, b, m: (0, n)),
                  pl.BlockSpec((1, tn), lambda n, b, m: (0, n)),
                  pl.BlockSpec((1, CONV_W - 1, tn), lambda n, b, m: (b, 0, n))],
        out_specs=[pl.BlockSpec((tm, tn), lambda n, b, m: (b * n_mt + m, n)),
                   pl.BlockSpec((1, CONV_W - 1, tn), lambda n, b, m: (b, 0, n))],
        out_shape=[jax.ShapeDtypeStruct((B * T, D_FF), bf16),
                   jax.ShapeDtypeStruct((B, CONV_W - 1, D_FF), f32)],
        scratch_shapes=[pltpu.VMEM((8, tn), f32), pltpu.VMEM((K, tn), bf16), pltpu.VMEM((K, tn), bf16)],
        compiler_params=_cparams(("arbitrary", "arbitrary", "arbitrary")),
        name="up_conv",
    )(x, w_up, w_up, conv_w, conv_b, buf)


def _pad_keys(a, lp):
    return jnp.pad(a, ((0, 0), (0, lp - a.shape[1]), (0, 0)))


def kernel(x_prompt, x_sample, cache_attn_k, cache_attn_v, cache_idx_k, state_ret, state_conv,
           w_in, b_gate, gn_g, w_out, ln1_g, ln1_b, w_up, conv_w, conv_b, w_down, ln2_g, ln2_b):
    Bp, Tp, _ = x_prompt.shape
    Bs, Ts, _ = x_sample.shape
    P = cache_attn_k.shape[2]
    Mp, Ms = Bp * Tp, Bs * Ts
    M = Mp + Ms
    x_all = jnp.concatenate([x_prompt.reshape(Mp, D_MODEL), x_sample.reshape(Ms, D_MODEL)], axis=0)
    xb = x_all.astype(bf16)

    splits = (R_HEADS * R_DK, R_HEADS * R_DK, D_MODEL, D_MODEL, A_HEADS * A_DH, A_KV_HEADS * A_DH,
              A_KV_HEADS * A_DH, I_HEADS * I_DH, I_DH, I_HEADS, D_MODEL, D_MODEL)
    off = np.concatenate([[0], np.cumsum(splits)])
    wi = w_in[0]
    col = lambda i: wi[:, off[i]:off[i + 1]]
    rq, rk, rv, rg, aq, ak, av, iq, ik, iw, g_r, g_a = range(12)
    w1 = jnp.concatenate([col(rq), col(rk), col(rv), col(aq), col(iq)], axis=1).astype(bf16)
    s1 = jnp.concatenate([jnp.ones((1024,), f32), jnp.full((1024,), R_DK ** -0.5, f32),
                          jnp.ones((2048,), f32), jnp.full((2048,), A_DH ** -0.5 * LOG2E, f32),
                          jnp.ones((1024,), f32)])[None]
    w2 = jnp.concatenate([col(rg), col(g_r), col(g_a)], axis=1).astype(bf16)
    w3 = jnp.concatenate([col(ak), col(av)], axis=1).astype(bf16)
    w4 = jnp.pad(jnp.concatenate([col(ik), col(iw)], axis=1),
                 ((0, 0), (0, LANES - I_DH - I_HEADS))).astype(bf16)
    g1 = _matmul(xb, w1, s1, bf16, 1024)
    g2 = _matmul(xb, w2, jnp.ones((1, w2.shape[1]), f32), f32, 1024)
    kv = _matmul(xb, w3, jnp.ones((1, w3.shape[1]), f32), f32, 512)
    ikw = _matmul(xb, w4, jnp.ones((1, LANES), f32), f32, LANES)

    b_r = b_gate[0, :D_MODEL][None]
    b_a = b_gate[0, D_MODEL:][None]
    gn = gn_g[0][None]

    yr_p, sn_p = _retention(g1, g2, gn, b_r, jnp.zeros((Bp, R_HEADS, R_DK, R_DV), f32), 0, Bp, Tp)
    yr_s, sn_s = _retention(g1, g2, gn, b_r, state_ret[0].astype(f32), Mp, Bs, Ts)

    kvb = kv.astype(bf16)
    ikb = ikw[:, :I_DH].astype(bf16)
    tq_p = _pick(Tp, (128,))
    tk_p = _pick(Tp, (2048, 1024, 512, 256, 128))
    ts_p = min(tk_p, 1024)
    mix_p = _dsa(g1, g2, ikw, ikb[:Mp].reshape(Bp, Tp, I_DH), kvb[:Mp, :512].reshape(Bp, Tp, 512),
                 kvb[:Mp, 512:].reshape(Bp, Tp, 512), yr_p, b_a, 0, Bp, Tp, 0, Tp, tq_p, tk_p, ts_p)
    Ls = P + Ts
    lp = -(-Ls // LANES) * LANES
    ki_s = jnp.concatenate([cache_idx_k[0].astype(bf16), ikb[Mp:].reshape(Bs, Ts, I_DH)], axis=1)
    k_s = jnp.concatenate([cache_attn_k[0].reshape(Bs, P, 512).astype(bf16),
                           kvb[Mp:, :512].reshape(Bs, Ts, 512)], axis=1)
    v_s = jnp.concatenate([cache_attn_v[0].reshape(Bs, P, 512).astype(bf16),
                           kvb[Mp:, 512:].reshape(Bs, Ts, 512)], axis=1)
    mix_s = _dsa(g1, g2, ikw, _pad_keys(ki_s, lp), _pad_keys(k_s, lp), _pad_keys(v_s, lp), yr_s, b_a,
                 Mp, Bs, Ts, P, Ls, Ts, lp, lp)

    wo, wd = w_out[0].astype(bf16), w_down[0].astype(bf16)
    g_1, b_1, g_2, b_2 = ln1_g[0][None], ln1_b[0][None], ln2_g[0][None], ln2_b[0][None]
    x1_p, x1b_p = _matmul_ln(mix_p, wo, x_prompt.reshape(Mp, D_MODEL), g_1, b_1)
    x1_s, x1b_s = _matmul_ln(mix_s, wo, x_sample.reshape(Ms, D_MODEL), g_1, b_1)
    ff_p, nb_p = _up_conv(x1b_p, w_up[0], conv_w[0], conv_b[0][None],
                          jnp.zeros((Bp, CONV_W - 1, D_FF), f32), 0, Bp, Tp)
    ff_s, nb_s = _up_conv(x1b_s, w_up[0], conv_w[0], conv_b[0][None], state_conv[0].astype(f32), 0, Bs, Ts)
    y_p, _ = _matmul_ln(ff_p, wd, x1_p, g_2, b_2)
    y_s, _ = _matmul_ln(ff_s, wd, x1_s, g_2, b_2)

    dt = x_prompt.dtype
    return (y_p.reshape(Bp, Tp, D_MODEL).astype(dt),
            y_s.reshape(Bs, Ts, D_MODEL).astype(dt),
            kv[:Mp, :512].reshape(1, Bp, Tp, A_KV_HEADS, A_DH).astype(dt),
            kv[:Mp, 512:].reshape(1, Bp, Tp, A_KV_HEADS, A_DH).astype(dt),
            ikw[:Mp, :I_DH].reshape(1, Bp, Tp, I_DH).astype(dt),
            sn_p[None].astype(dt),
            nb_p[None].astype(dt),
            kv[Mp:, :512].reshape(1, Bs, Ts, A_KV_HEADS, A_DH).astype(dt),
            kv[Mp:, 512:].reshape(1, Bs, Ts, A_KV_HEADS, A_DH).astype(dt),
            ikw[Mp:, :I_DH].reshape(1, Bs, Ts, I_DH).astype(dt),
            sn_s[None].astype(dt),
            nb_s[None].astype(dt))
```

```python
import functools

import numpy as np
import jax
import jax.numpy as jnp
from jax import lax
from jax.experimental import pallas as pl
from jax.experimental.pallas import tpu as pltpu

f32 = jnp.float32
bf16 = jnp.bfloat16

D_MODEL = 2048
CHUNK = 64
R_HEADS = 8
R_DK = 128
R_DV = 256
A_HEADS = 16
A_KV_HEADS = 4
A_DH = 128
A_GROUP = 4
I_HEADS = 16
I_DH = 64
TOPK_MAX = 256
D_FF = 5632
CONV_W = 3
DEPTH = 1
ALPHA = (2 * DEPTH) ** 0.25
LN_EPS = 1e-5

LANES = 128
VMEM_LIMIT = 56 * 1024 * 1024
NEG_BIG = -1e30
F32_LOWEST = -3.4028234663852886e38
INT_MIN = -2 ** 31
LOG2E = 1.4426950408889634
JCUT_ALL = 2 ** 30
UNDERFLOW_LOG2 = 160.0


def _pick(n, cands):
    for c in cands:
        if n % c == 0:
            return c
    raise ValueError(f"no tile in {cands} divides {n}")


def _cparams(sem):
    return pltpu.CompilerParams(dimension_semantics=sem, vmem_limit_bytes=VMEM_LIMIT)


def _sigmoid(x):
    return 1.0 / (1.0 + jnp.exp(-x))


def _rows_bf16_kernel(xp_ref, xs_ref, o_ref, *, n_p):
    m = pl.program_id(0)

    @pl.when(m < n_p)
    def _():
        o_ref[...] = xp_ref[...].astype(bf16)

    @pl.when(m >= n_p)
    def _():
        o_ref[...] = xs_ref[...].astype(bf16)


def _rows_bf16(xp, xs):
    Mp, D = xp.shape
    Ms = xs.shape[0]
    tm = _pick(np.gcd(Mp, Ms), (512, 256, 128, 64))
    n_p, n_s = Mp // tm, Ms // tm
    return pl.pallas_call(
        functools.partial(_rows_bf16_kernel, n_p=n_p),
        grid=(n_p + n_s,),
        in_specs=[pl.BlockSpec((tm, D), lambda m: (jnp.minimum(m, n_p - 1), 0)),
                  pl.BlockSpec((tm, D), lambda m: (jnp.maximum(m - n_p, 0), 0))],
        out_specs=pl.BlockSpec((tm, D), lambda m: (m, 0)),
        out_shape=jax.ShapeDtypeStruct((Mp + Ms, D), bf16),
        compiler_params=_cparams(("parallel",)),
        name="rows_bf16",
    )(xp, xs)


def _mm_kernel(x_ref, w_ref, s_ref, o_ref):
    acc = jnp.dot(x_ref[...], w_ref[...], preferred_element_type=f32)
    o_ref[...] = (acc * s_ref[...]).astype(o_ref.dtype)


def _matmul(x, w, scale, out_dtype, tn):
    M, K = x.shape
    N = w.shape[1]
    tm = _pick(M, (1536, 1024, 512, 384, 256, 128, 64))
    return pl.pallas_call(
        _mm_kernel,
        grid=(N // tn, M // tm),
        in_specs=[pl.BlockSpec((tm, K), lambda n, m: (m, 0)),
                  pl.BlockSpec((K, tn), lambda n, m: (0, n)),
                  pl.BlockSpec((1, tn), lambda n, m: (0, n))],
        out_specs=pl.BlockSpec((tm, tn), lambda n, m: (m, n)),
        out_shape=jax.ShapeDtypeStruct((M, N), out_dtype),
        compiler_params=_cparams(("parallel", "parallel")),
        name="in_proj",
    )(x, w, scale)


def _kv_kernel(x_ref, w_ref, k_ref, v_ref, kb_ref, vb_ref):
    acc = jnp.dot(x_ref[...], w_ref[...], preferred_element_type=f32)
    half = acc.shape[1] // 2
    k_ref[...] = acc[:, :half]
    v_ref[...] = acc[:, half:]
    kb_ref[...] = acc[:, :half].astype(bf16)
    vb_ref[...] = acc[:, half:].astype(bf16)


def _kv_proj(x, w, row0, rows):
    K = x.shape[1]
    half = w.shape[1] // 2
    tm = _pick(rows, (1024, 512, 256, 128, 64))
    rb0 = row0 // tm
    return pl.pallas_call(
        _kv_kernel,
        grid=(rows // tm,),
        in_specs=[pl.BlockSpec((tm, K), lambda m: (rb0 + m, 0)),
                  pl.BlockSpec((K, 2 * half), lambda m: (0, 0))],
        out_specs=[pl.BlockSpec((tm, half), lambda m: (m, 0))] * 4,
        out_shape=[jax.ShapeDtypeStruct((rows, half), f32)] * 2 + [jax.ShapeDtypeStruct((rows, half), bf16)] * 2,
        compiler_params=_cparams(("parallel",)),
        name="kv_proj",
    )(x, w)


def _ret_kernel(q_ref, k_ref, v_ref, rg_ref, gr_ref, dm_ref, cd_ref, kd_ref, gc_ref, gn_ref, br_ref,
                s0_ref, y_ref, sn_ref, s_scr, *, n_chunks):
    c = pl.program_id(1)

    @pl.when(c == 0)
    def _():
        s_scr[...] = s0_ref[0]

    for h in range(R_HEADS):
        q = q_ref[:, h * R_DK:(h + 1) * R_DK]
        k = k_ref[:, h * R_DK:(h + 1) * R_DK]
        v = v_ref[:, h * R_DV:(h + 1) * R_DV]
        s_old = s_scr[h]
        sc = lax.dot_general(q, k, (((1,), (1,)), ((), ())), preferred_element_type=f32) * dm_ref[h]
        inner = jnp.dot(sc.astype(bf16), v, preferred_element_type=f32)
        cross = jnp.dot(q, s_old.astype(bf16), preferred_element_type=f32) * cd_ref[h]
        o = inner + cross
        kdk = (k.astype(f32) * kd_ref[h]).astype(bf16)
        upd = lax.dot_general(kdk, v, (((0,), (0,)), ((), ())), preferred_element_type=f32)
        s_scr[h] = gc_ref[h] * s_old + upd
        mu = jnp.mean(o, axis=-1, keepdims=True)
        d = o - mu
        var = jnp.mean(d * d, axis=-1, keepdims=True)
        cols = slice(h * R_DV, (h + 1) * R_DV)
        on = d * lax.rsqrt(var + LN_EPS) * gn_ref[:, cols]
        rg = rg_ref[:, cols]
        gate = _sigmoid(gr_ref[:, cols] + br_ref[:, cols])
        y_ref[:, cols] = gate * (rg * _sigmoid(rg) * on)

    @pl.when(c == n_chunks - 1)
    def _():
        sn_ref[0] = s_scr[...]


def _retention_tables(C):
    lg = np.log1p(-np.exp2(-5.0 - np.arange(R_HEADS, dtype=np.float64)))
    i = np.arange(C, dtype=np.float64)
    diff = i[:, None] - i[None, :]
    dm = np.where(diff >= 0, np.exp(lg[:, None, None] * np.maximum(diff, 0.0)), 0.0)
    cd = np.exp((i + 1.0)[None, :] * lg[:, None])
    kd = np.exp((C - 1.0 - i)[None, :] * lg[:, None])
    gc = np.exp(C * lg)
    return (jnp.asarray(dm, f32),
            jnp.asarray(np.broadcast_to(cd[:, :, None], (R_HEADS, C, R_DV)), f32),
            jnp.asarray(np.broadcast_to(kd[:, :, None], (R_HEADS, C, R_DK)), f32),
            jnp.asarray(np.broadcast_to(gc[:, None, None], (R_HEADS, 1, R_DV)), f32))


def _retention(g1, g2, gn_g, b_r, s0, row0, B, T):
    C = min(T, 256)
    n_chunks = T // C
    rb0 = row0 // C
    dm, cd, kd, gc = _retention_tables(C)
    rows = lambda b, c: rb0 + b * n_chunks + c
    const3 = lambda b, c: (0, 0, 0)
    return pl.pallas_call(
        functools.partial(_ret_kernel, n_chunks=n_chunks),
        grid=(B, n_chunks),
        in_specs=[pl.BlockSpec((C, 1024), lambda b, c: (rows(b, c), 0)),
                  pl.BlockSpec((C, 1024), lambda b, c: (rows(b, c), 1)),
                  pl.BlockSpec((C, 2048), lambda b, c: (rows(b, c), 1)),
                  pl.BlockSpec((C, 2048), lambda b, c: (rows(b, c), 0)),
                  pl.BlockSpec((C, 2048), lambda b, c: (rows(b, c), 1)),
                  pl.BlockSpec((R_HEADS, C, C), const3),
                  pl.BlockSpec((R_HEADS, C, R_DV), const3),
                  pl.BlockSpec((R_HEADS, C, R_DK), const3),
                  pl.BlockSpec((R_HEADS, 1, R_DV), const3),
                  pl.BlockSpec((1, D_MODEL), lambda b, c: (0, 0)),
                  pl.BlockSpec((1, D_MODEL), lambda b, c: (0, 0)),
                  pl.BlockSpec((1, R_HEADS, R_DK, R_DV), lambda b, c: (b, 0, 0, 0))],
        out_specs=[pl.BlockSpec((C, D_MODEL), lambda b, c: (b * n_chunks + c, 0)),
                   pl.BlockSpec((1, R_HEADS, R_DK, R_DV), lambda b, c: (b, 0, 0, 0))],
        out_shape=[jax.ShapeDtypeStruct((B * T, D_MODEL), f32),
                   jax.ShapeDtypeStruct((B, R_HEADS, R_DK, R_DV), f32)],
        scratch_shapes=[pltpu.VMEM((R_HEADS, R_DK, R_DV), f32)],
        compiler_params=_cparams(("parallel", "arbitrary")),
        name="retention",
    )(g1, g1, g1, g2, g2, dm, cd, kd, gc, gn_g, b_r, s0)


def _bf16_split3(x):
    parts = []
    rem = np.float64(x)
    for _ in range(3):
        p = np.float64(np.asarray(rem, np.float32).astype(jnp.bfloat16).astype(np.float32))
        parts.append(p)
        rem = rem - p
    return parts


def _alibi_tables(TK):
    r = 1
    while TK // r > 256:
        r *= 2
    jj = np.arange(TK)
    kx = np.zeros((TK, LANES), np.float32)
    kx[:, 0:3] = (jj // r)[:, None]
    kx[:, 3:6] = (jj % r)[:, None]
    sig = np.zeros((A_HEADS, LANES), np.float32)
    sig_eff = []
    for h in range(A_HEADS):
        s3 = _bf16_split3(2.0 ** (-8.0 * (h + 1) / A_HEADS) * LOG2E)
        sig[h, 0:3] = [r * p for p in s3]
        sig[h, 3:6] = s3
        sig_eff.append(float(sum(s3)))
    vx = np.zeros((TK, LANES), np.float32)
    vx[:, 0] = 1.0
    return jnp.asarray(kx, bf16), jnp.asarray(vx, bf16), jnp.asarray(sig, f32), tuple(sig_eff)


def _key_to_float(u):
    key = u ^ jnp.int32(INT_MIN)
    bits = key ^ ((key >> 31) & jnp.int32(0x7FFFFFFF))
    return lax.bitcast_convert_type(bits, f32)


def _upper_half(x):
    bits = lax.bitcast_convert_type(x, jnp.int32) & jnp.int32(-65536)
    return lax.bitcast_convert_type(bits, f32)


def _knorm_kernel(k_ref, o_ref):
    rows = []
    for g in range(A_KV_HEADS):
        kg = k_ref[:, g * A_DH:(g + 1) * A_DH].astype(f32)
        n2 = jnp.max(jnp.sum(kg * kg, axis=1, keepdims=True), axis=0, keepdims=True)
        rows.append(jnp.broadcast_to(jnp.sqrt(n2), (1, LANES)))
    rows.append(jnp.zeros((8 - A_KV_HEADS, LANES), f32))
    o_ref[...] = jnp.concatenate(rows, axis=0)


def _knorm(k_all, TK):
    B, Lp, _ = k_all.shape
    nkb = Lp // TK
    return pl.pallas_call(
        _knorm_kernel,
        grid=(B, nkb),
        in_specs=[pl.BlockSpec((None, TK, 512), lambda b, kb: (b, kb, 0))],
        out_specs=pl.BlockSpec((None, 8, LANES), lambda b, kb: (b, kb, 0)),
        out_shape=jax.ShapeDtypeStruct((B, nkb * 8, LANES), f32),
        compiler_params=_cparams(("parallel", "parallel")),
        name="knorm",
    )(k_all)


def _dsa_kernel(si_ref, skb_ref, aq_ref, iq_ref, ikw_ref, kia_ref, k_ref, v_ref, kx_ref, vx_ref, sig_ref,
                km_ref, yr_ref, ga_ref, ba_ref, o_ref,
                qi_scr, qg_scr, isc_scr, hi_scr, thr_scr, jcut_scr, qn_scr, m_scr, acc_scr,
                *, TQ, TK, TS, L, P, n_sel, sig_eff, idx_bits):
    t = pl.program_id(1)
    i = si_ref[t]
    kb = skb_ref[t]
    r = TK // TS
    q0 = P + i * TQ
    lim_max = jnp.minimum(((q0 + TQ - 1) // CHUNK + 1) * CHUNK, L)
    nkb = (lim_max + TK - 1) // TK
    nsb = (lim_max + TS - 1) // TS
    pos = q0 + lax.broadcasted_iota(jnp.int32, (TQ, 1), 0)
    lim = jnp.minimum((lax.shift_right_logical(pos, 6) + 1) * CHUNK, L)

    def count_rows(pred):
        lane = lax.broadcasted_iota(jnp.int32, (TQ, LANES), 1)

        def block(sb, cnt):
            x = isc_scr[sb]
            for c in range(TS // LANES):
                hit = pred(x[:, c * LANES:(c + 1) * LANES], sb * TS + c * LANES + lane)
                cnt = cnt + jnp.where(hit, 1.0, 0.0)
            return cnt

        cnt = lax.fori_loop(0, nsb, block, jnp.zeros((TQ, LANES), f32))
        return jnp.sum(cnt, axis=1, keepdims=True)

    @pl.when(kb == nkb - 1)
    def _prepare():
        for h in range(I_HEADS):
            qi_scr[h * TQ:(h + 1) * TQ, :] = iq_ref[:, h * I_DH:(h + 1) * I_DH]
        for g in range(A_KV_HEADS):
            for hg in range(A_GROUP):
                h = g * A_GROUP + hg
                qh = aq_ref[:, h * A_DH:(h + 1) * A_DH]
                qg_scr[g, hg * TQ:(hg + 1) * TQ, 0:A_DH] = qh
                qg_scr[g, hg * TQ:(hg + 1) * TQ, A_DH:2 * A_DH] = jnp.broadcast_to(
                    sig_ref[h:h + 1, :], (TQ, LANES)).astype(bf16)
                qf = qh.astype(f32)
                qn_scr[h] = jnp.broadcast_to(jnp.sqrt(jnp.sum(qf * qf, axis=1, keepdims=True)), (TQ, LANES))
        w = ikw_ref[:, I_DH:I_DH + I_HEADS] * (I_DH ** -0.5 * I_HEADS ** -0.5)

        def score_block(sb, carry):
            start = pl.multiple_of(sb * TS, TS)
            ki = kia_ref[pl.ds(start, TS), :]
            logits = lax.dot_general(qi_scr[...], ki, (((1,), (1,)), ((), ())),
                                     preferred_element_type=f32)
            isc = jnp.zeros((TQ, TS), f32)
            for h in range(I_HEADS):
                isc = isc + jnp.maximum(logits[h * TQ:(h + 1) * TQ], 0.0) * w[:, h:h + 1]
            j = start + lax.broadcasted_iota(jnp.int32, (TQ, TS), 1)
            isc = jnp.where(j < lim, isc, -jnp.inf)
            isc_scr[sb] = isc
            hi_scr[sb] = _upper_half(isc).astype(bf16)
            return carry

        lax.fori_loop(0, nsb, score_block, 0)

        def fill_block(sb, carry):
            isc_scr[sb] = jnp.full((TQ, TS), -jnp.inf, f32)
            return carry

        lax.fori_loop(nsb, nkb * r, fill_block, 0)

        def hi_pass(b, carry):
            u, cu = carry
            cand = u | lax.shift_left(jnp.int32(1), 31 - b)
            tb = jnp.broadcast_to(_upper_half(_key_to_float(cand)), (TQ, LANES)).astype(bf16)

            def block(sb, cnt):
                x = hi_scr[sb]
                for c in range(TS // LANES):
                    cnt = cnt + jnp.where(x[:, c * LANES:(c + 1) * LANES] >= tb, one_b, zero_b)
                return cnt

            one_b, zero_b = jnp.ones((), bf16), jnp.zeros((), bf16)
            cnt = lax.fori_loop(0, nsb, block, jnp.zeros((TQ, LANES), bf16))
            total = jnp.sum(cnt.astype(f32), axis=1, keepdims=True)
            ok = total >= float(n_sel)
            return jnp.where(ok, cand, u), jnp.where(ok, total, cu)

        def bit_pass(b, carry):
            u, cu = carry
            cand = u | lax.shift_left(jnp.int32(1), 31 - b)
            tf = _key_to_float(cand)
            total = count_rows(lambda x, j: x >= tf)
            ok = total >= float(n_sel)
            return jnp.where(ok, cand, u), jnp.where(ok, total, cu)

        carry = lax.fori_loop(0, 16, hi_pass, (jnp.zeros((TQ, 1), jnp.int32), jnp.zeros((TQ, 1), f32)))
        u, cu = lax.fori_loop(16, 32, bit_pass, carry)
        thr = jnp.where(lim <= n_sel, F32_LOWEST, _key_to_float(u))
        thr_scr[...] = jnp.broadcast_to(thr, (TQ, LANES))
        jcut_scr[...] = jnp.full((TQ, LANES), JCUT_ALL, jnp.int32)
        m_scr[...] = jnp.full(m_scr.shape, NEG_BIG, f32)
        acc_scr[...] = jnp.zeros(acc_scr.shape, f32)

        tied = jnp.logical_and(lim > n_sel, cu > float(n_sel))

        @pl.when(jnp.max(jnp.where(tied, 1.0, 0.0)) > 0.0)
        def _ties():
            room = float(n_sel) - count_rows(lambda x, j: x > thr)

            def idx_pass(b, jc):
                cand = jc | lax.shift_left(jnp.int32(1), idx_bits - 1 - b)
                c = count_rows(lambda x, j: jnp.logical_and(x == thr, j < cand))
                return jnp.where(c <= room, cand, jc)

            jc = lax.fori_loop(0, idx_bits, idx_pass, jnp.zeros((TQ, 1), jnp.int32))
            jcut_scr[...] = jnp.broadcast_to(jnp.where(tied, jc, JCUT_ALL), (TQ, LANES))

    def attend(diag, g_first, ncol=TK):
        x = jnp.concatenate([isc_scr[kb * r + c] for c in range(ncol // TS)], axis=1)
        thr = thr_scr[:, 0:1]
        j = kb * TK + lax.broadcasted_iota(jnp.int32, (TQ, ncol), 1)
        sel = jnp.logical_or(x > thr, jnp.logical_and(x == thr, j < jcut_scr[:, 0:1]))
        if diag:
            ahead = jnp.maximum(j - pos, 0).astype(f32)
        back = jnp.full((TQ, 1), kb * TK - q0, jnp.int32).astype(f32)
        PAIR = A_GROUP // 2
        for g in range(g_first, A_KV_HEADS):
            kg = jnp.concatenate([k_ref[0:ncol, g * A_DH:(g + 1) * A_DH], kx_ref[0:ncol, :]], axis=1)
            vg = jnp.concatenate([v_ref[0:ncol, g * A_DH:(g + 1) * A_DH], vx_ref[0:ncol, :]], axis=1)
            for half in range(A_GROUP // PAIR):
                rows = slice(half * PAIR * TQ, (half + 1) * PAIR * TQ)
                s = lax.dot_general(qg_scr[g, rows, :], kg, (((1,), (1,)), ((), ())),
                                    preferred_element_type=f32)
                ps, alphas = [], []
                for hp in range(PAIR):
                    h = g * A_GROUP + half * PAIR + hp
                    sh = s[hp * TQ:(hp + 1) * TQ]
                    if diag:
                        sh = sh - (2.0 * sig_eff[h]) * ahead
                    sh = jnp.where(sel, sh, NEG_BIG)
                    d = back * sig_eff[h]
                    m_prev = m_scr[h]
                    m_new = jnp.maximum(m_prev, jnp.max(sh, axis=1, keepdims=True) + d)
                    ps.append(jnp.exp2(sh - (m_new[:, 0:1] - d)))
                    alphas.append(jnp.exp2(m_prev - m_new))
                    m_scr[h] = m_new
                p = jnp.concatenate(ps, axis=0).astype(bf16)
                a = jnp.concatenate(alphas, axis=0)
                acc_scr[g, rows, :] = (jnp.concatenate([a, a], axis=1) * acc_scr[g, rows, :]
                                       + jnp.dot(p, vg, preferred_element_type=f32))

    is_diag = kb * TK + TK - 1 > q0

    half_ok = (TK // 2) % TS == 0
    short = jnp.logical_and(half_ok, lim_max - kb * TK <= TK // 2)

    @pl.when(jnp.logical_and(is_diag, jnp.logical_not(short)))
    def _():
        attend(True, 0)

    if half_ok:
        @pl.when(jnp.logical_and(is_diag, short))
        def _():
            attend(True, 0, TK // 2)

    @pl.when(jnp.logical_not(is_diag))
    def _():
        back1 = jnp.full((1, LANES), kb * TK - q0, jnp.int32).astype(f32)

        def needed(g):
            worst = None
            for hg in range(A_GROUP):
                h = g * A_GROUP + hg
                over = (qn_scr[h] * (km_ref[g:g + 1, :] * 1.001)
                        + (1.0 + sig_eff[h] * (TK - 1) + UNDERFLOW_LOG2) + back1 * sig_eff[h] - m_scr[h])
                worst = over if worst is None else jnp.maximum(worst, over)
            return jnp.max(worst) > 0.0

        g_first = jnp.int32(A_KV_HEADS - 1)
        for g in range(A_KV_HEADS - 2, -1, -1):
            g_first = jnp.where(needed(g), g, g_first)
        for gs in range(A_KV_HEADS):
            @pl.when(g_first == gs)
            def _():
                attend(False, gs)

    @pl.when(kb == 0)
    def _finish():
        gate = _sigmoid(ga_ref[...] + ba_ref[...])
        for g in range(A_KV_HEADS):
            acc = acc_scr[g]
            og = acc[:, 0:A_DH] / acc[:, A_DH:A_DH + 1]
            for hg in range(A_GROUP):
                cols = slice((g * A_GROUP + hg) * A_DH, (g * A_GROUP + hg + 1) * A_DH)
                ya = og[hg * TQ:(hg + 1) * TQ]
                o_ref[:, cols] = (yr_ref[:, cols] + gate[:, cols] * ya).astype(o_ref.dtype)


def _dsa(g1, g2, ikw, kia, k_all, v_all, yr, b_a, row0, B, T, P, L, TQ, TK, TS):
    Lp = kia.shape[1]
    nkb_total = Lp // TK
    nq = T // TQ
    rb0 = row0 // TQ
    n_sel = min(TOPK_MAX, L // 4)
    steps = []
    for i in range(nq):
        lim_max = min(((P + i * TQ + TQ - 1) // CHUNK + 1) * CHUNK, L)
        steps += [(i, kb) for kb in reversed(range(-(-lim_max // TK)))]
    step_i = jnp.asarray([s[0] for s in steps], jnp.int32)
    step_kb = jnp.asarray([s[1] for s in steps], jnp.int32)
    kx, vx, sig, sig_eff = _alibi_tables(TK)
    kmax = _knorm(k_all, TK)
    rows = lambda b, t, si: rb0 + b * nq + si[t]
    kern = functools.partial(_dsa_kernel, TQ=TQ, TK=TK, TS=TS, L=L, P=P, n_sel=n_sel, sig_eff=sig_eff,
                             idx_bits=int(Lp).bit_length())
    grid_spec = pltpu.PrefetchScalarGridSpec(
        num_scalar_prefetch=2,
        grid=(B, len(steps)),
        in_specs=[pl.BlockSpec((TQ, 2048), lambda b, t, si, sk: (rows(b, t, si), 2)),
                  pl.BlockSpec((TQ, 1024), lambda b, t, si, sk: (rows(b, t, si), 6)),
                  pl.BlockSpec((TQ, LANES), lambda b, t, si, sk: (rows(b, t, si), 0)),
                  pl.BlockSpec((None, Lp, I_DH), lambda b, t, si, sk: (b, 0, 0)),
                  pl.BlockSpec((None, TK, 512), lambda b, t, si, sk: (b, sk[t], 0)),
                  pl.BlockSpec((None, TK, 512), lambda b, t, si, sk: (b, sk[t], 0)),
                  pl.BlockSpec((TK, LANES), lambda b, t, si, sk: (0, 0)),
                  pl.BlockSpec((TK, LANES), lambda b, t, si, sk: (0, 0)),
                  pl.BlockSpec((A_HEADS, LANES), lambda b, t, si, sk: (0, 0)),
                  pl.BlockSpec((None, 8, LANES), lambda b, t, si, sk: (b, sk[t], 0)),
                  pl.BlockSpec((TQ, D_MODEL), lambda b, t, si, sk: (b * nq + si[t], 0)),
                  pl.BlockSpec((TQ, 2048), lambda b, t, si, sk: (rows(b, t, si), 2)),
                  pl.BlockSpec((1, D_MODEL), lambda b, t, si, sk: (0, 0))],
        out_specs=pl.BlockSpec((TQ, D_MODEL), lambda b, t, si, sk: (b * nq + si[t], 0)),
        scratch_shapes=[pltpu.VMEM((I_HEADS * TQ, I_DH), bf16),
                        pltpu.VMEM((A_KV_HEADS, A_GROUP * TQ, 2 * A_DH), bf16),
                        pltpu.VMEM((nkb_total * (TK // TS), TQ, TS), f32),
                        pltpu.VMEM((nkb_total * (TK // TS), TQ, TS), bf16),
                        pltpu.VMEM((TQ, LANES), f32),
                        pltpu.VMEM((TQ, LANES), jnp.int32),
                        pltpu.VMEM((A_HEADS, TQ, LANES), f32),
                        pltpu.VMEM((A_HEADS, TQ, LANES), f32),
                        pltpu.VMEM((A_KV_HEADS, A_GROUP * TQ, 2 * A_DH), f32)])
    return pl.pallas_call(
        kern,
        grid_spec=grid_spec,
        out_shape=jax.ShapeDtypeStruct((B * T, D_MODEL), bf16),
        compiler_params=_cparams(("parallel", "arbitrary")),
        name="dsa",
    )(step_i, step_kb, g1, g1, ikw, kia, k_all, v_all, kx, vx, sig, kmax, yr, g2, b_a)


def _mm_ln_kernel(x_ref, w_ref, r_ref, g_ref, b_ref, o_ref, ob_ref, acc_scr, *, nk):
    kk = pl.program_id(1)

    @pl.when(kk == 0)
    def _():
        acc_scr[...] = jnp.zeros(acc_scr.shape, f32)

    acc_scr[...] += jnp.dot(x_ref[...], w_ref[...], preferred_element_type=f32)

    @pl.when(kk == nk - 1)
    def _():
        y = ALPHA * r_ref[...] + acc_scr[...]
        mu = jnp.mean(y, axis=-1, keepdims=True)
        d = y - mu
        var = jnp.mean(d * d, axis=-1, keepdims=True)
        out = d * lax.rsqrt(var + LN_EPS) * g_ref[...] + b_ref[...]
        o_ref[...] = out
        ob_ref[...] = out.astype(bf16)


def _matmul_ln(x, w, resid, gamma, beta):
    M, K = x.shape
    N = w.shape[1]
    tm = _pick(M, (512, 384, 256, 128, 64))
    tk = _pick(K, (2048, 1408, 512, 256, 128))
    nk = K // tk
    return pl.pallas_call(
        functools.partial(_mm_ln_kernel, nk=nk),
        grid=(M // tm, nk),
        in_specs=[pl.BlockSpec((tm, tk), lambda m, k: (m, k)),
                  pl.BlockSpec((tk, N), lambda m, k: (k, 0)),
                  pl.BlockSpec((tm, N), lambda m, k: (m, 0)),
                  pl.BlockSpec((1, N), lambda m, k: (0, 0)),
                  pl.BlockSpec((1, N), lambda m, k: (0, 0))],
        out_specs=[pl.BlockSpec((tm, N), lambda m, k: (m, 0)),
                   pl.BlockSpec((tm, N), lambda m, k: (m, 0))],
        out_shape=[jax.ShapeDtypeStruct((M, N), f32), jax.ShapeDtypeStruct((M, N), bf16)],
        scratch_shapes=[pltpu.VMEM((tm, N), f32)],
        compiler_params=_cparams(("parallel", "arbitrary")),
        name="proj_ln",
    )(x, w, resid, gamma, beta)


def _gelu_tanh(x):
    return 0.5 * x * (1.0 + jnp.tanh(0.7978845608028654 * (x + 0.044715 * (x * x * x))))


def _up_kernel(x_ref, wh_ref, wu_ref, cw_ref, cb_ref, buf_ref, o_ref, nb_ref, carry_scr, wh_scr, wu_scr,
               *, tm, n_mt):
    m = pl.program_id(2)

    @pl.when(jnp.logical_and(pl.program_id(1) == 0, m == 0))
    def _():
        wh_scr[...] = wh_ref[...].astype(bf16)
        wu_scr[...] = wu_ref[...].astype(bf16)

    @pl.when(m == 0)
    def _():
        carry_scr[0:8 - (CONV_W - 1), :] = jnp.zeros((8 - (CONV_W - 1), carry_scr.shape[1]), f32)
        carry_scr[8 - (CONV_W - 1):8, :] = buf_ref[0]

    x = x_ref[...]
    h = jnp.dot(x, wh_scr[...], preferred_element_type=f32)
    u = jnp.dot(x, wu_scr[...], preferred_element_type=f32)
    prev = carry_scr[...]
    row = lax.broadcasted_iota(jnp.int32, h.shape, 0)
    hc = cb_ref[...] + cw_ref[CONV_W - 1:CONV_W, :] * h
    for s in range(1, CONV_W):
        shifted = pltpu.roll(h, s, 0)
        head = pltpu.roll(prev, s, 0)
        head = jnp.concatenate([head] * (tm // 8), axis=0)
        shifted = jnp.where(row < s, head, shifted)
        hc = hc + cw_ref[CONV_W - 1 - s:CONV_W - s, :] * shifted
    o_ref[...] = (_gelu_tanh(hc) * u).astype(o_ref.dtype)
    carry_scr[...] = h[tm - 8:tm, :]

    @pl.when(m == n_mt - 1)
    def _():
        nb_ref[0] = h[tm - (CONV_W - 1):tm, :]


def _up_conv(x, w_up, conv_w, conv_b, buf, row0, B, T):
    K = x.shape[1]
    tm = _pick(T, (1024, 512, 256, 128, 64))
    tn = 512
    n_mt = T // tm
    rb0 = row0 // tm
    return pl.pallas_call(
        functools.partial(_up_kernel, tm=tm, n_mt=n_mt),
        grid=(D_FF // tn, B, n_mt),
        in_specs=[pl.BlockSpec((tm, K), lambda n, b, m: (rb0 + b * n_mt + m, 0)),
                  pl.BlockSpec((K, tn), lambda n, b, m: (0, n)),
                  pl.BlockSpec((K, tn), lambda n, b, m: (0, n + D_FF // tn)),
                  pl.BlockSpec((CONV_W, tn), lambda n, b, m: (0, n)),
                  pl.BlockSpec((1, tn), lambda n, b, m: (0, n)),
                  pl.BlockSpec((1, CONV_W - 1, tn), lambda n, b, m: (b, 0, n))],
        out_specs=[pl.BlockSpec((tm, tn), lambda n, b, m: (b * n_mt + m, n)),
                   pl.BlockSpec((1, CONV_W - 1, tn), lambda n, b, m: (b, 0, n))],
        out_shape=[jax.ShapeDtypeStruct((B * T, D_FF), bf16),
                   jax.ShapeDtypeStruct((B, CONV_W - 1, D_FF), f32)],
        scratch_shapes=[pltpu.VMEM((8, tn), f32), pltpu.VMEM((K, tn), bf16), pltpu.VMEM((K, tn), bf16)],
        compiler_params=_cparams(("arbitrary", "arbitrary", "arbitrary")),
        name="up_conv",
    )(x, w_up, w_up, conv_w, conv_b, buf)


def _up_seq_kernel(x_ref, wh_ref, wu_ref, cw_ref, cb_ref, buf_ref, o_ref, nb_ref, *, B, T):
    x = x_ref[...]
    h = jnp.dot(x, wh_ref[...].astype(bf16), preferred_element_type=f32)
    u = jnp.dot(x, wu_ref[...].astype(bf16), preferred_element_type=f32)
    tn = h.shape[1]
    t_in_seq = lax.broadcasted_iota(jnp.int32, h.shape, 0) % T
    hc = cb_ref[...] + cw_ref[CONV_W - 1:CONV_W, :] * h
    for s in range(1, CONV_W):
        shifted = pltpu.roll(h, s, 0)
        pieces = []
        for b in range(B):
            tail = jnp.concatenate([jnp.zeros((8 - (CONV_W - 1), tn), f32), buf_ref[b]], axis=0)
            pieces += [pltpu.roll(tail, s, 0), jnp.zeros((T - 8, tn), f32)]
        shifted = jnp.where(t_in_seq < s, jnp.concatenate(pieces, axis=0), shifted)
        hc = hc + cw_ref[CONV_W - 1 - s:CONV_W - s, :] * shifted
    o_ref[...] = (_gelu_tanh(hc) * u).astype(o_ref.dtype)
    for b in range(B):
        nb_ref[b] = h[(b + 1) * T - (CONV_W - 1):(b + 1) * T, :]


def _up_conv_seq(x, w_up, conv_w, conv_b, buf, B, T):
    K = x.shape[1]
    tn = 512
    return pl.pallas_call(
        functools.partial(_up_seq_kernel, B=B, T=T),
        grid=(D_FF // tn,),
        in_specs=[pl.BlockSpec((B * T, K), lambda n: (0, 0)),
                  pl.BlockSpec((K, tn), lambda n: (0, n)),
                  pl.BlockSpec((K, tn), lambda n: (0, n + D_FF // tn)),
                  pl.BlockSpec((CONV_W, tn), lambda n: (0, n)),
                  pl.BlockSpec((1, tn), lambda n: (0, n)),
                  pl.BlockSpec((B, CONV_W - 1, tn), lambda n: (0, 0, n))],
        out_specs=[pl.BlockSpec((B * T, tn), lambda n: (0, n)),
                   pl.BlockSpec((B, CONV_W - 1, tn), lambda n: (0, 0, n))],
        out_shape=[jax.ShapeDtypeStruct((B * T, D_FF), bf16),
                   jax.ShapeDtypeStruct((B, CONV_W - 1, D_FF), f32)],
        compiler_params=_cparams(("parallel",)),
        name="up_conv_seq",
    )(x, w_up, w_up, conv_w, conv_b, buf)


def _pad_keys(a, lp):
    return jnp.pad(a, ((0, 0), (0, lp - a.shape[1]), (0, 0)))


def kernel(x_prompt, x_sample, cache_attn_k, cache_attn_v, cache_idx_k, state_ret, state_conv,
           w_in, b_gate, gn_g, w_out, ln1_g, ln1_b, w_up, conv_w, conv_b, w_down, ln2_g, ln2_b):
    Bp, Tp, _ = x_prompt.shape
    Bs, Ts, _ = x_sample.shape
    P = cache_attn_k.shape[2]
    Mp, Ms = Bp * Tp, Bs * Ts
    M = Mp + Ms
    xb = _rows_bf16(x_prompt.reshape(Mp, D_MODEL), x_sample.reshape(Ms, D_MODEL))

    splits = (R_HEADS * R_DK, R_HEADS * R_DK, D_MODEL, D_MODEL, A_HEADS * A_DH, A_KV_HEADS * A_DH,
              A_KV_HEADS * A_DH, I_HEADS * I_DH, I_DH, I_HEADS, D_MODEL, D_MODEL)
    off = np.concatenate([[0], np.cumsum(splits)])
    wi = w_in[0]
    col = lambda i: wi[:, off[i]:off[i + 1]]
    rq, rk, rv, rg, aq, ak, av, iq, ik, iw, g_r, g_a = range(12)
    w1 = jnp.concatenate([col(rq), col(rk), col(rv), col(aq), col(iq)], axis=1).astype(bf16)
    s1 = jnp.concatenate([jnp.ones((1024,), f32), jnp.full((1024,), R_DK ** -0.5, f32),
                          jnp.ones((2048,), f32), jnp.full((2048,), A_DH ** -0.5 * LOG2E, f32),
                          jnp.ones((1024,), f32)])[None]
    w2 = jnp.concatenate([col(rg), col(g_r), col(g_a)], axis=1).astype(bf16)
    w3 = jnp.concatenate([col(ak), col(av)], axis=1).astype(bf16)
    w4 = jnp.pad(jnp.concatenate([col(ik), col(iw)], axis=1),
                 ((0, 0), (0, LANES - I_DH - I_HEADS))).astype(bf16)
    g1 = _matmul(xb, w1, s1, bf16, 1024)
    g2 = _matmul(xb, w2, jnp.ones((1, w2.shape[1]), f32), f32, 1024)
    k_p, v_p, kb_p, vb_p = _kv_proj(xb, w3, 0, Mp)
    k_s, v_s, kb_s, vb_s = _kv_proj(xb, w3, Mp, Ms)
    ikw = _matmul(xb, w4, jnp.ones((1, LANES), f32), f32, LANES)

    b_r = b_gate[0, :D_MODEL][None]
    b_a = b_gate[0, D_MODEL:][None]
    gn = gn_g[0][None]

    yr_p, sn_p = _retention(g1, g2, gn, b_r, jnp.zeros((Bp, R_HEADS, R_DK, R_DV), f32), 0, Bp, Tp)
    yr_s, sn_s = _retention(g1, g2, gn, b_r, state_ret[0].astype(f32), Mp, Bs, Ts)

    ikb = ikw[:, :I_DH].astype(bf16)
    tq_p = _pick(Tp, (128,))
    tk_p = _pick(Tp, (2048, 1024, 512, 256, 128))
    ts_p = min(tk_p, 1024)
    mix_p = _dsa(g1, g2, ikw, ikb[:Mp].reshape(Bp, Tp, I_DH), kb_p.reshape(Bp, Tp, 512),
                 vb_p.reshape(Bp, Tp, 512), yr_p, b_a, 0, Bp, Tp, 0, Tp, tq_p, tk_p, ts_p)
    Ls = P + Ts
    lp = -(-Ls // LANES) * LANES
    ki_all = jnp.concatenate([cache_idx_k[0].astype(bf16), ikb[Mp:].reshape(Bs, Ts, I_DH)], axis=1)
    k_all = jnp.concatenate([cache_attn_k[0].reshape(Bs, P, 512).astype(bf16), kb_s.reshape(Bs, Ts, 512)], axis=1)
    v_all = jnp.concatenate([cache_attn_v[0].reshape(Bs, P, 512).astype(bf16), vb_s.reshape(Bs, Ts, 512)], axis=1)
    mix_s = _dsa(g1, g2, ikw, _pad_keys(ki_all, lp), _pad_keys(k_all, lp), _pad_keys(v_all, lp), yr_s, b_a,
                 Mp, Bs, Ts, P, Ls, Ts, lp, lp)

    wo, wd = w_out[0].astype(bf16), w_down[0].astype(bf16)
    g_1, b_1, g_2, b_2 = ln1_g[0][None], ln1_b[0][None], ln2_g[0][None], ln2_b[0][None]
    x1_p, x1b_p = _matmul_ln(mix_p, wo, x_prompt.reshape(Mp, D_MODEL), g_1, b_1)
    x1_s, x1b_s = _matmul_ln(mix_s, wo, x_sample.reshape(Ms, D_MODEL), g_1, b_1)
    ff_p, nb_p = _up_conv(x1b_p, w_up[0], conv_w[0], conv_b[0][None],
                          jnp.zeros((Bp, CONV_W - 1, D_FF), f32), 0, Bp, Tp)
    ff_s, nb_s = _up_conv_seq(x1b_s, w_up[0], conv_w[0], conv_b[0][None], state_conv[0].astype(f32), Bs, Ts)
    y_p, _ = _matmul_ln(ff_p, wd, x1_p, g_2, b_2)
    y_s, _ = _matmul_ln(ff_s, wd, x1_s, g_2, b_2)

    dt = x_prompt.dtype
    return (y_p.reshape(Bp, Tp, D_MODEL).astype(dt),
            y_s.reshape(Bs, Ts, D_MODEL).astype(dt),
            k_p.reshape(1, Bp, Tp, A_KV_HEADS, A_DH).astype(dt),
            v_p.reshape(1, Bp, Tp, A_KV_HEADS, A_DH).astype(dt),
            ikw[:Mp, :I_DH].reshape(1, Bp, Tp, I_DH).astype(dt),
            sn_p[None].astype(dt),
            nb_p[None].astype(dt),
            k_s.reshape(1, Bs, Ts, A_KV_HEADS, A_DH).astype(dt),
            v_s.reshape(1, Bs, Ts, A_KV_HEADS, A_DH).astype(dt),
            ikw[Mp:, :I_DH].reshape(1, Bs, Ts, I_DH).astype(dt),
            sn_s[None].astype(dt),
            nb_s[None].astype(dt))
```
